```python
import jax, jax.numpy as jnp
from jax import lax
import numpy as np

D_MODEL = 2048
BATCH = 2
SEQ = 4096
DEPTH = 2
DEC_BATCH = 128
DEC_SEQ = 8
PAST_LEN = 8192
PAGE_SIZE = 128

A_HEADS = 8
A_KV = 2
A_G = A_HEADS // A_KV
A_HD = 128
CMP_STRIDE = 16
CMP_LEN = 2 * CMP_STRIDE
CMP_HID = A_HD
SEL_BLOCK = 64
SEL_TOPK = 16
SEL_LOCAL = 2
SEL_QBLOCK = 64
A_WIN = 512
B_CH = 1024
CONV_W = 3
POOL_WINDOWS = (2, 4, 8, 16)
C_CH = 1024
C_GRP = C_CH // len(POOL_WINDOWS)
POOL_STATE = max(POOL_WINDOWS) - 1
D_HEADS = 16
D_KV = 2
D_G = D_HEADS // D_KV
D_HD = 64
D_WIN = 128
D_FF = 5632
N_SUB = 3
Q_BLOCK = 128
RMS_EPS = 1e-6
NEG = -1e30
FORCED = 1e9
TINY = 1e-30

A_Q = A_HEADS * A_HD
A_KVW = 2 * A_KV * A_HD
A_GATES = 3 * A_HEADS
L0_COLS = A_Q + 3 * A_KVW + A_GATES + 3 * B_CH
MIX0 = A_Q + B_CH
D_Q = D_HEADS * D_HD
D_KVW = D_KV * D_HD
L1_COLS = D_Q + 2 * D_KVW + C_CH
MIX1 = D_Q + C_CH

kernel_name = 'hybrid_nsa_conv_pool_swa_decoder_step'


def alibi_slopes(n_heads, n_kv):
    s = 2.0 ** (-8.0 * np.arange(1, n_heads + 1) / n_heads)
    return jnp.asarray(s.reshape(n_kv, n_heads // n_kv), dtype=jnp.float32)


def rmsnorm(x, g):
    xf = x.astype(jnp.float32)
    xf = xf * lax.rsqrt(jnp.mean(xf * xf, axis=-1, keepdims=True) + RMS_EPS)
    return (xf * g.astype(jnp.float32)).astype(x.dtype)


def modnorm(x, g, mod):
    return rmsnorm(x, g) * (1.0 + mod[:, 1]) + mod[:, 0]


def swiglu(h, w_in, w_out):
    g, u = jnp.split(h @ w_in, 2, axis=-1)
    return (jax.nn.silu(g) * u) @ w_out


def last_rows(a, n):
    t = a.shape[1]
    if t < n:
        a = jnp.pad(a, [(0, 0), (n - t, 0)] + [(0, 0)] * (a.ndim - 2))
    return a[:, a.shape[1] - n:]


def masked_softmax(s, valid, sink=None):
    s = jnp.where(valid, s, NEG)
    m = jnp.max(s, axis=-1, keepdims=True)
    if sink is not None:
        m = jnp.maximum(m, sink)
    p = jnp.where(valid, jnp.exp(s - m), 0.0)
    den = jnp.sum(p, axis=-1, keepdims=True)
    if sink is not None:
        den = den + jnp.exp(sink - m)
    return p / jnp.maximum(den, TINY)


def attn_core(q, k, v, qpos, kpos, slopes, window=None, sink=None):
    s = jnp.einsum('nmqkgd,nmskd->nmkgqs', q, k, preferred_element_type=jnp.float32) * (q.shape[-1] ** -0.5)
    dist = (qpos[:, :, None] - kpos[:, None, :])[None, :, None, None]
    valid = (dist >= 0) & (kpos[:, None, :] >= 0)[None, :, None, None]
    if window is not None:
        valid = valid & (dist < window)
    s = s - slopes[None, None, :, :, None, None] * dist.astype(jnp.float32)
    sk = None if sink is None else sink.astype(jnp.float32).reshape(slopes.shape)[None, None, :, :, None, None]
    p = masked_softmax(s, valid, sk)
    o = jnp.einsum('nmkgqs,nmskd->nmqkgd', p.astype(v.dtype), v)
    return o, p


def banded_attn(q, k, v, window, slopes, sink=None):
    n, t = q.shape[:2]
    nb = t // Q_BLOCK
    nw = -(-window // Q_BLOCK)

    def band(a):
        a = a.reshape(n, nb, Q_BLOCK, *a.shape[2:])
        a = jnp.pad(a, [(0, 0), (nw, 0)] + [(0, 0)] * (a.ndim - 2))
        return jnp.concatenate([a[:, j:j + nb] for j in range(nw + 1)], axis=2)

    qpos = jnp.arange(t).reshape(nb, Q_BLOCK)
    kpos = (jnp.arange(nb)[:, None] - nw) * Q_BLOCK + jnp.arange((nw + 1) * Q_BLOCK)[None]
    o, _ = attn_core(q.reshape(n, nb, Q_BLOCK, *q.shape[2:]), band(k), band(v), qpos, kpos, slopes, window, sink)
    return o.reshape(q.shape)


def window_attn_sample(q, buf, kv_new, window, slopes, sink=None):
    w = buf.shape[1]
    t = q.shape[1]
    kv = jnp.concatenate([buf, kv_new], axis=1)
    qpos = (PAST_LEN + jnp.arange(t))[None]
    kpos = (PAST_LEN - w + jnp.arange(w + t))[None]
    o, _ = attn_core(q[:, None], kv[:, None, :, 0], kv[:, None, :, 1], qpos, kpos, slopes, window, sink)
    return o[:, 0], kv[:, t:]


def cmp_halves(rows, w_cmp1, pe_cmp):
    n, l = rows.shape[:2]
    h = rows.reshape(n, l // CMP_STRIDE, CMP_STRIDE, 2, A_KV, A_HD)
    proj = jnp.einsum('nmjtkd,tpjde->nmtpke', h, w_cmp1)
    pe_proj = jnp.einsum('tpjd,tpjde->tpe', pe_cmp, w_cmp1)
    return proj + pe_proj[None, None, :, :, None, :]


def cmp_halves_paged(cache_cmp, page_table, w_cmp1, pe_cmp):
    n, n_pages = page_table.shape
    h = lax.map(lambda pages: cmp_halves(cache_cmp[pages], w_cmp1, pe_cmp), page_table.T)
    h = jnp.moveaxis(h, 0, 1)
    return h.reshape(n, n_pages * h.shape[2], *h.shape[3:])


def cmp_summaries(halves, w_cmp2):
    pre = halves[:, :-1, :, 0] + halves[:, 1:, :, 1]
    summ = jnp.einsum('nctke,tef->nctkf', jax.nn.gelu(pre), w_cmp2)
    end = jnp.arange(summ.shape[1]) * CMP_STRIDE + (CMP_LEN - 1)
    return summ[:, :, 0], summ[:, :, 1], end


def select_blocks(p_cmp, qpos, n_s):
    r = SEL_BLOCK // CMP_STRIDE
    p = jnp.sum(p_cmp, axis=2)
    p = jnp.pad(p, [(0, 0), (0, 0), (0, 0), (0, r * n_s - p.shape[-1])])
    pr = p.reshape(*p.shape[:-1], n_s, r)
    tail = pr[..., r - 1]
    score = jnp.sum(pr[..., :r - 1], axis=-1) + 0.5 * (tail + jnp.pad(tail[..., :-1], [(0, 0)] * 3 + [(1, 0)]))
    j = jnp.arange(n_s)
    cur = (qpos // SEL_BLOCK)[:, None]
    forced = (j == 0) | ((cur - j >= 0) & (cur - j < SEL_LOCAL))
    score = jnp.where(forced, FORCED, jnp.where(j <= cur, score, -1.0))
    _, idx = lax.top_k(score, min(SEL_TOPK, n_s))
    return idx


def sel_attn_block(q, idx, qpos, kg, vg, slopes):
    kpos = idx[..., None] * SEL_BLOCK + jnp.arange(SEL_BLOCK)
    s = jnp.einsum('nqkgd,nkqjbd->nkgqjb', q, kg, preferred_element_type=jnp.float32) * (q.shape[-1] ** -0.5)
    dist = (qpos[None, None, :, None, None] - kpos)[:, :, None]
    s = s - slopes[None, :, :, None, None, None] * dist.astype(jnp.float32)
    p = masked_softmax(s.reshape(*s.shape[:4], -1), (dist >= 0).reshape(*dist.shape[:4], -1)).reshape(s.shape)
    return jnp.einsum('nkgqjb,nkqjbd->nqkgd', p.astype(vg.dtype), vg)


def sel_attn_prompt(q, idx, kv_sel, slopes):
    b, t = q.shape[:2]
    n_s = t // SEL_BLOCK
    nq = t // SEL_QBLOCK
    blocks = jnp.moveaxis(kv_sel.reshape(b, n_s, SEL_BLOCK, 2, A_KV, A_HD), 4, 1)
    b_i = jnp.arange(b)[:, None, None]
    kv_i = jnp.arange(A_KV)[None, :, None]

    def step(args):
        qb, ib, pb = args
        g = blocks[b_i, kv_i, ib.reshape(b, A_KV, -1)]
        g = g.reshape(b, A_KV, SEL_QBLOCK, ib.shape[-1], SEL_BLOCK, 2, A_HD)
        return sel_attn_block(qb, ib, pb, g[..., 0, :], g[..., 1, :], slopes)

    qs = jnp.moveaxis(q.reshape(b, nq, SEL_QBLOCK, A_KV, A_G, A_HD), 1, 0)
    ids = jnp.moveaxis(idx.reshape(b, A_KV, nq, SEL_QBLOCK, -1), 2, 0)
    ps = jnp.arange(t).reshape(nq, SEL_QBLOCK)
    o = lax.map(step, (qs, ids, ps))
    return jnp.moveaxis(o, 0, 1).reshape(b, t, A_KV, A_G, A_HD)


def sel_attn_sample(q, idx, cache_sel, kv_new, page_table, slopes):
    n, t = q.shape[:2]
    bpp = PAGE_SIZE // SEL_BLOCK
    n_past = PAST_LEN // SEL_BLOCK
    n_new = -(-t // SEL_BLOCK)
    pool_blocks = cache_sel.reshape(-1, SEL_BLOCK, 2, A_KV, A_HD)
    new_blocks = jnp.pad(kv_new, [(0, 0), (0, n_new * SEL_BLOCK - t), (0, 0), (0, 0), (0, 0)])
    new_blocks = new_blocks.reshape(n, n_new, SEL_BLOCK, 2, A_KV, A_HD)
    n_i = jnp.arange(n)[:, None, None, None]
    kv_i = jnp.arange(A_KV)[None, :, None, None]

    def step(args):
        qt, it, pt = args
        jp = jnp.minimum(it, n_past - 1)
        phys = page_table[n_i, jp // bpp] * bpp + jp % bpp
        g_past = pool_blocks[phys, :, :, kv_i]
        g_new = new_blocks[n_i, jnp.clip(it - n_past, 0, n_new - 1), :, :, kv_i]
        g = jnp.where((it < n_past)[..., None, None, None], g_past, g_new)
        return sel_attn_block(qt, it, pt, g[..., 0, :], g[..., 1, :], slopes)

    qs = jnp.moveaxis(q[:, :, None], 1, 0)
    ids = jnp.moveaxis(idx[:, :, :, None], 2, 0)
    ps = (PAST_LEN + jnp.arange(t))[:, None]
    o = lax.map(step, (qs, ids, ps))
    return jnp.moveaxis(o, 0, 1).reshape(n, t, A_KV, A_G, A_HD)


def causal_conv(v_ext, w_conv):
    t = v_ext.shape[1] - (CONV_W - 1)
    out = w_conv[0] * v_ext[:, 0:t]
    for i in range(1, CONV_W):
        out = out + w_conv[i] * v_ext[:, i:i + t]
    return out


def split_l0(z):
    n, t = z.shape[:2]
    q = z[..., :A_Q].reshape(n, t, A_KV, A_G, A_HD)
    o = A_Q
    kvs = [z[..., o + i * A_KVW:o + (i + 1) * A_KVW].reshape(n, t, 2, A_KV, A_HD) for i in range(3)]
    o = o + 3 * A_KVW
    gates = jax.nn.sigmoid(z[..., o:o + A_GATES].astype(jnp.float32)).reshape(n, t, A_KV, A_G, 3).astype(z.dtype)
    o = o + A_GATES
    u = z[..., o:o + B_CH]
    bg = z[..., o + B_CH:o + 2 * B_CH]
    cg = z[..., o + 2 * B_CH:o + 3 * B_CH]
    return q, kvs[0], kvs[1], kvs[2], gates, u, bg, cg


def nsa_combine(gates, o_c, o_s, o_w):
    return gates[..., 0:1] * o_c + gates[..., 1:2] * o_s + gates[..., 2:3] * o_w


def mix0_prompt(h, w_in0, w_out0, w_cmp1, pe_cmp, w_cmp2, w_conv):
    n, t, _ = h.shape
    slopes = alibi_slopes(A_HEADS, A_KV)
    q, kv_c, kv_s, kv_w, gates, u, bg, cg = split_l0(h @ w_in0)
    qpos = jnp.arange(t)
    k_c, v_c, end = cmp_summaries(cmp_halves(kv_c, w_cmp1, pe_cmp), w_cmp2)
    o_c, p_c = attn_core(q[:, None], k_c[:, None], v_c[:, None], qpos[None], end[None], slopes)
    idx = select_blocks(p_c[:, 0], qpos, -(-t // SEL_BLOCK))
    o_s = sel_attn_prompt(q, idx, kv_s, slopes)
    o_w = banded_attn(q, kv_w[:, :, 0], kv_w[:, :, 1], A_WIN, slopes)
    o_a = nsa_combine(gates, o_c[:, 0], o_s, o_w).reshape(n, t, A_Q)
    v = cg * u
    y_b = bg * causal_conv(jnp.pad(v, [(0, 0), (CONV_W - 1, 0), (0, 0)]), w_conv)
    y = jnp.concatenate([o_a, y_b], axis=-1) @ w_out0
    return y, (kv_c, kv_s, last_rows(kv_w, A_WIN), last_rows(v, CONV_W - 1))


def mix0_sample(h, cache_cmp_kv, cache_sel_kv, state_win_kv, state_conv, page_table,
                w_in0, w_out0, w_cmp1, pe_cmp, w_cmp2, w_conv):
    n, t, _ = h.shape
    slopes = alibi_slopes(A_HEADS, A_KV)
    q, kv_c, kv_s, kv_w, gates, u, bg, cg = split_l0(h @ w_in0)
    qpos = PAST_LEN + jnp.arange(t)
    t_pad = -(-t // CMP_STRIDE) * CMP_STRIDE
    halves = jnp.concatenate([
        cmp_halves_paged(cache_cmp_kv, page_table, w_cmp1, pe_cmp),
        cmp_halves(jnp.pad(kv_c, [(0, 0), (0, t_pad - t), (0, 0), (0, 0), (0, 0)]), w_cmp1, pe_cmp)], axis=1)
    k_c, v_c, end = cmp_summaries(halves, w_cmp2)
    o_c, p_c = attn_core(q[:, None], k_c[:, None], v_c[:, None], qpos[None], end[None], slopes)
    idx = select_blocks(p_c[:, 0], qpos, -(-(PAST_LEN + t) // SEL_BLOCK))
    o_s = sel_attn_sample(q, idx, cache_sel_kv, kv_s, page_table, slopes)
    o_w, new_win = window_attn_sample(q, state_win_kv, kv_w, A_WIN, slopes)
    o_a = nsa_combine(gates, o_c[:, 0], o_s, o_w).reshape(n, t, A_Q)
    v_ext = jnp.concatenate([state_conv, cg * u], axis=1)
    y_b = bg * causal_conv(v_ext, w_conv)
    y = jnp.concatenate([o_a, y_b], axis=-1) @ w_out0
    return y, (kv_c, kv_s, new_win, v_ext[:, v_ext.shape[1] - (CONV_W - 1):])


def pool_mix(u_ext, pos, w_pool, pool_scale):
    t = pos.shape[0]
    p0 = u_ext.shape[1] - t
    cs = jnp.pad(jnp.cumsum(u_ext.astype(jnp.float32), axis=1), [(0, 0), (1, 0), (0, 0)])
    outs = []
    for g, w in enumerate(POOL_WINDOWS):
        lo, hi = g * C_GRP, (g + 1) * C_GRP
        win_sum = cs[:, p0 + 1:p0 + 1 + t, lo:hi] - cs[:, p0 + 1 - w:p0 + 1 - w + t, lo:hi]
        cnt = jnp.minimum(w, pos + 1).astype(jnp.float32)[None, :, None]
        d = win_sum / cnt - u_ext[:, p0:, lo:hi].astype(jnp.float32)
        outs.append(jnp.einsum('ntc,ce->nte', d.astype(u_ext.dtype), w_pool[g]))
    return jnp.concatenate(outs, axis=-1) * pool_scale


def split_l1(z):
    n, t = z.shape[:2]
    q = z[..., :D_Q].reshape(n, t, D_KV, D_G, D_HD)
    kv = z[..., D_Q:D_Q + 2 * D_KVW].reshape(n, t, 2, D_KV, D_HD)
    u = z[..., D_Q + 2 * D_KVW:]
    return q, kv, u


def mix1_prompt(h, w_in1, w_out1, attn_sinks, w_pool, pool_scale):
    n, t, _ = h.shape
    q, kv, u = split_l1(h @ w_in1)
    o_d = banded_attn(q, kv[:, :, 0], kv[:, :, 1], D_WIN, alibi_slopes(D_HEADS, D_KV), attn_sinks).reshape(n, t, D_Q)
    y_c = pool_mix(jnp.pad(u, [(0, 0), (POOL_STATE, 0), (0, 0)]), jnp.arange(t), w_pool, pool_scale)
    y = jnp.concatenate([o_d, y_c], axis=-1) @ w_out1
    return y, (last_rows(u, POOL_STATE), last_rows(kv, D_WIN))


def mix1_sample(h, state_pool, state_swa_kv, w_in1, w_out1, attn_sinks, w_pool, pool_scale):
    n, t, _ = h.shape
    q, kv, u = split_l1(h @ w_in1)
    o_d, new_swa = window_attn_sample(q, state_swa_kv, kv, D_WIN, alibi_slopes(D_HEADS, D_KV), attn_sinks)
    u_ext = jnp.concatenate([state_pool, u], axis=1)
    y_c = pool_mix(u_ext, PAST_LEN + jnp.arange(t), w_pool, pool_scale)
    y = jnp.concatenate([o_d.reshape(n, t, D_Q), y_c], axis=-1) @ w_out1
    return y, (u_ext[:, u_ext.shape[1] - POOL_STATE:], new_swa)


def run_trunk(x, c, mixers, norm_g, w_ada, b_ada, w_ffn_in, w_ffn_out, final_g):
    n = c.shape[0]
    states = []
    for l in range(DEPTH):
        mods = (jax.nn.silu(c) @ w_ada[l] + b_ada[l]).reshape(n, N_SUB, 3, 1, D_MODEL)
        x = x + 0.5 * mods[:, 0, 2] * swiglu(modnorm(x, norm_g[l, 0], mods[:, 0]), w_ffn_in[l, 0], w_ffn_out[l, 0])
        y, st = mixers[l % 2](modnorm(x, norm_g[l, 1], mods[:, 1]))
        x = x + mods[:, 1, 2] * y
        x = x + 0.5 * mods[:, 2, 2] * swiglu(modnorm(x, norm_g[l, 2], mods[:, 2]), w_ffn_in[l, 1], w_ffn_out[l, 1])
        states.append(st)
    return rmsnorm(x, final_g), states


def setup_inputs(seed: int = 0) -> dict:
    key = jax.random.key(seed)
    ks = iter(jax.random.split(key, 32))

    def nrm(shape, scale):
        return jax.random.normal(next(ks), shape, jnp.float32) * scale

    d = D_MODEL
    n_pages = PAST_LEN // PAGE_SIZE
    n_pool = (5 * DEC_BATCH * n_pages + 3) // 4
    page_table = jax.random.permutation(next(ks), n_pool)[:DEC_BATCH * n_pages].reshape(DEC_BATCH, n_pages).astype(jnp.int32)
    return {
        'x_prompt': nrm((BATCH, SEQ, d), 1.0),
        'x_sample': nrm((DEC_BATCH, DEC_SEQ, d), 1.0),
        'cache_cmp_kv': nrm((n_pool, PAGE_SIZE, 2, A_KV, A_HD), 1.0),
        'cache_sel_kv': nrm((n_pool, PAGE_SIZE, 2, A_KV, A_HD), 1.0),
        'state_win_kv': nrm((DEC_BATCH, A_WIN, 2, A_KV, A_HD), 1.0),
        'state_conv': nrm((DEC_BATCH, CONV_W - 1, B_CH), 1.0),
        'state_pool': nrm((DEC_BATCH, POOL_STATE, C_CH), 1.0),
        'state_swa_kv': nrm((DEC_BATCH, D_WIN, 2, D_KV, D_HD), 1.0),
        'page_table': page_table,
        'c_prompt': nrm((BATCH, d), 1.0),
        'c_sample': nrm((DEC_BATCH, d), 1.0),
        'norm_g': 1.0 + nrm((DEPTH, N_SUB, d), 0.05),
        'w_ada': nrm((DEPTH, d, N_SUB * 3 * d), 0.5 * d ** -0.5),
        'b_ada': nrm((DEPTH, N_SUB * 3 * d), 0.02),
        'w_ffn_in': nrm((DEPTH, 2, d, 2 * D_FF), d ** -0.5),
        'w_ffn_out': nrm((DEPTH, 2, D_FF, d), D_FF ** -0.5),
        'final_g': 1.0 + nrm((d,), 0.05),
        'w_in0': nrm((d, L0_COLS), d ** -0.5),
        'w_out0': nrm((MIX0, d), MIX0 ** -0.5),
        'w_cmp1': nrm((2, 2, CMP_STRIDE, A_HD, CMP_HID), (CMP_LEN * A_HD) ** -0.5),
        'pe_cmp': nrm((2, 2, CMP_STRIDE, A_HD), 0.1),
        'w_cmp2': nrm((2, CMP_HID, A_HD), 1.5 * CMP_HID ** -0.5),
        'w_conv': nrm((CONV_W, B_CH), CONV_W ** -0.5),
        'w_in1': nrm((d, L1_COLS), d ** -0.5),
        'w_out1': nrm((MIX1, d), MIX1 ** -0.5),
        'attn_sinks': nrm((D_HEADS,), 0.5),
        'w_pool': nrm((len(POOL_WINDOWS), C_GRP, C_GRP), C_GRP ** -0.5),
        'pool_scale': 1.0 + nrm((C_CH,), 0.05),
    }


def reference(x_prompt, x_sample, cache_cmp_kv, cache_sel_kv, state_win_kv, state_conv, state_pool, state_swa_kv,
              page_table, c_prompt, c_sample, norm_g, w_ada, b_ada, w_ffn_in, w_ffn_out, final_g,
              w_in0, w_out0, w_cmp1, pe_cmp, w_cmp2, w_conv, w_in1, w_out1, attn_sinks, w_pool, pool_scale):
    even_p = lambda h: mix0_prompt(h, w_in0, w_out0, w_cmp1, pe_cmp, w_cmp2, w_conv)
    odd_p = lambda h: mix1_prompt(h, w_in1, w_out1, attn_sinks, w_pool, pool_scale)
    even_s = lambda h: mix0_sample(h, cache_cmp_kv, cache_sel_kv, state_win_kv, state_conv, page_table,
                                   w_in0, w_out0, w_cmp1, pe_cmp, w_cmp2, w_conv)
    odd_s = lambda h: mix1_sample(h, state_pool, state_swa_kv, w_in1, w_out1, attn_sinks, w_pool, pool_scale)
    y_prompt, st_p = run_trunk(x_prompt, c_prompt, (even_p, odd_p), norm_g, w_ada, b_ada, w_ffn_in, w_ffn_out, final_g)
    y_sample, st_s = run_trunk(x_sample, c_sample, (even_s, odd_s), norm_g, w_ada, b_ada, w_ffn_in, w_ffn_out, final_g)
    (cmp_p, sel_p, win_p, conv_p), (pool_p, swa_p) = st_p
    (cmp_s, sel_s, win_s, conv_s), (pool_s, swa_s) = st_s
    return (y_prompt, y_sample, cmp_p, cmp_s, sel_p, sel_s, win_p, win_s, conv_p, conv_s, pool_p, pool_s, swa_p, swa_s)
```

```python
import functools

import numpy as np
import jax
import jax.numpy as jnp
from jax import lax
from jax.experimental import pallas as pl
from jax.experimental.pallas import tpu as pltpu

F32 = jnp.float32
BF16 = jnp.bfloat16
SDS = jax.ShapeDtypeStruct
BS = pl.BlockSpec

A_HEADS, A_KV, A_HD = 8, 2, 128
A_G = A_HEADS // A_KV
CMP_STRIDE = 16
SEL_BLOCK, SEL_TOPK, SEL_LOCAL = 64, 16, 2
A_WIN = 512
B_CH, CONV_W = 1024, 3
POOL_WINDOWS = (2, 4, 8, 16)
C_CH = 1024
C_GRP = C_CH // len(POOL_WINDOWS)
POOL_STATE = max(POOL_WINDOWS) - 1
D_HEADS, D_KV, D_HD = 16, 2, 64
D_G = D_HEADS // D_KV
D_WIN = 128
N_SUB = 3
RMS_EPS = 1e-6
NEG = -1e30
FORCED = 1e9
TINY = 1e-30
A_Q = A_HEADS * A_HD
A_KVW = 2 * A_KV * A_HD
A_GATES = 3 * A_HEADS
D_Q = D_HEADS * D_HD
D_KVW = D_KV * D_HD

LANES = 128
L0_U, L0_B, L0_C = A_Q, A_Q + B_CH, A_Q + 2 * B_CH
L0_KVC = A_Q + 3 * B_CH
L0_KVS = L0_KVC + A_KVW
L0_KVW = L0_KVS + A_KVW
L0_GT = L0_KVW + A_KVW
L0_N = L0_GT + LANES
L1_U = D_Q
L1_KV = D_Q + C_CH
L1_N = L1_KV + 2 * D_KVW

VMEM_LIMIT = 56 * 2**20


def _cp(n_axes):
    return pltpu.CompilerParams(dimension_semantics=("arbitrary",) * n_axes, vmem_limit_bytes=VMEM_LIMIT)


def _pick(n, cands):
    for c in cands:
        if n % c == 0:
            return c
    return n


def _slopes(n_heads, n_kv):
    s = 2.0 ** (-8.0 * np.arange(1, n_heads + 1) / n_heads)
    return [[float(v) for v in row] for row in s.reshape(n_kv, n_heads // n_kv)]


def _rows(m, tm):
    r = m.shape[0]
    if r == 1 or r == tm:
        return m
    return jnp.concatenate([m] * (tm // r), axis=0)


def _col(vals, reps):
    return jnp.concatenate([jnp.full((reps, 1), v, F32) for v in vals], axis=0)


def _modnorm(x, g, scale, shift, tm):
    xn = x * lax.rsqrt(jnp.mean(x * x, axis=-1, keepdims=True) + RMS_EPS) * g
    return xn * (1.0 + _rows(scale, tm)) + _rows(shift, tm)


def _nt_dot(a, b):
    return lax.dot_general(a, b, (((1,), (1,)), ((), ())), preferred_element_type=F32)


def _split3_dot(p, a):
    p1 = p.astype(BF16)
    r1 = p - p1.astype(F32)
    p2 = r1.astype(BF16)
    p3 = (r1 - p2.astype(F32)).astype(BF16)
    d = lambda t: jnp.dot(t, a, preferred_element_type=F32)
    return d(p1) + d(p2) + d(p3)


def _topk_mask(score, k):
    lane = lax.broadcasted_iota(jnp.int32, score.shape, 1).astype(F32)
    big = float(score.shape[1])

    def body(_, carry):
        sc, sel = carry
        m = jnp.max(sc, axis=1, keepdims=True)
        idx = jnp.min(jnp.where(sc == m, lane, big), axis=1, keepdims=True)
        pick = lane == idx
        return jnp.where(pick, -3e38, sc), jnp.where(pick, 1.0, sel)

    _, sel = lax.fori_loop(0, k, body, (score, jnp.zeros_like(score)))
    return sel


def _select_scores(score, qpos0):
    j = lax.broadcasted_iota(jnp.int32, score.shape, 1)
    cur = (qpos0 + lax.broadcasted_iota(jnp.int32, score.shape, 0)) // SEL_BLOCK
    forced = (j == 0) | ((cur - j >= 0) & (cur - j < SEL_LOCAL))
    return jnp.where(forced, FORCED, jnp.where(j <= cur, score, -1.0))


def _sel_matrix(nc, ns_pad):
    r = SEL_BLOCK // CMP_STRIDE
    a = np.zeros((nc, ns_pad), np.float32)
    for i in range(nc):
        j = i // r
        if i % r < r - 1:
            a[i, j] = 1.0
        else:
            a[i, j] = 0.5
            if j + 1 < ns_pad:
                a[i, j + 1] = 0.5
    return jnp.asarray(a, BF16)


def _ada_kernel(c_ref, w_ref, b_ref, o_ref):
    c = c_ref[...]
    a = (c * jax.nn.sigmoid(c)).astype(BF16)
    o_ref[...] = jnp.dot(a, w_ref[...].astype(BF16), preferred_element_type=F32) + b_ref[...]


def _ada(c_all, w_ada, b_ada):
    nl, d, n9 = w_ada.shape
    m = c_all.shape[0]
    tn = _pick(n9, (1024, 512, 256, 128))
    return pl.pallas_call(
        _ada_kernel,
        out_shape=SDS((nl, m, n9), F32),
        grid=(nl, n9 // tn),
        in_specs=[BS((m, d), lambda l, j: (0, 0)),
                  BS((None, d, tn), lambda l, j: (l, 0, j)),
                  BS((None, 1, tn), lambda l, j: (l, 0, j))],
        out_specs=BS((None, m, tn), lambda l, j: (l, 0, j)),
        compiler_params=_cp(2),
    )(c_all, w_ada, b_ada.reshape(nl, 1, n9))


class _Group:
    def __init__(self, rows, tm, mods, spec_fn):
        self.rows, self.tm, self.mods, self._spec_fn = rows, tm, mods, spec_fn

    def mod_spec(self, sub, k):
        return self._spec_fn(3 * sub + k)


def _prompt_group(mods_b, b, t, d, tm):
    tpb = t // tm
    arr = mods_b.reshape(b * 3 * N_SUB, 1, d)
    return _Group(b * t, tm, arr, lambda idx: BS((None, 1, d), lambda i, *_: ((i // tpb) * (3 * N_SUB) + idx, 0, 0)))


def _sample_group(mods_n, n, ts, d, tm):
    arr = mods_n.reshape(n, 3 * N_SUB, d).transpose(1, 0, 2)
    return _Group(n * ts, tm, arr, lambda idx: BS((None, n, d), lambda i, *_: (idx, 0, 0)))


def _ffn_kernel(x_ref, sh_ref, sc_ref, gt_ref, g_ref, wg_ref, wu_ref, wo_ref, o_ref, h_ref, *, nf):
    f = pl.program_id(1)
    tm = x_ref.shape[0]

    @pl.when(f == 0)
    def _():
        h_ref[...] = _modnorm(x_ref[...], g_ref[...], sc_ref[...], sh_ref[...], tm).astype(BF16)
        o_ref[...] = jnp.zeros_like(o_ref)

    h = h_ref[...]
    a = jnp.dot(h, wg_ref[...], preferred_element_type=F32)
    u = jnp.dot(h, wu_ref[...], preferred_element_type=F32)
    act = (a * jax.nn.sigmoid(a) * u).astype(BF16)
    o_ref[...] += jnp.dot(act, wo_ref[...], preferred_element_type=F32)

    @pl.when(f == nf - 1)
    def _():
        o_ref[...] = x_ref[...] + 0.5 * _rows(gt_ref[...], tm) * o_ref[...]


def _ffn(x, grp, sub, g, w_in, w_out):
    d = x.shape[1]
    dff = w_out.shape[0]
    tf = _pick(dff, (512, 256, 128))
    nf = dff // tf
    tm = grp.tm
    return pl.pallas_call(
        functools.partial(_ffn_kernel, nf=nf),
        out_shape=SDS(x.shape, F32),
        grid=(grp.rows // tm, nf),
        in_specs=[BS((tm, d), lambda i, f: (i, 0)),
                  grp.mod_spec(sub, 0), grp.mod_spec(sub, 1), grp.mod_spec(sub, 2),
                  BS((1, d), lambda i, f: (0, 0)),
                  BS((d, tf), lambda i, f: (0, f)),
                  BS((d, tf), lambda i, f: (0, nf + f)),
                  BS((tf, d), lambda i, f: (f, 0))],
        out_specs=BS((tm, d), lambda i, f: (i, 0)),
        scratch_shapes=[pltpu.VMEM((tm, d), BF16)],
        compiler_params=_cp(2),
    )(x, grp.mods, grp.mods, grp.mods, g.reshape(1, d), w_in, w_in, w_out)


def _proj_kernel(x_ref, sh_ref, sc_ref, g_ref, w_ref, o_ref, h_ref):
    tm = x_ref.shape[0]

    @pl.when(pl.program_id(1) == 0)
    def _():
        h_ref[...] = _modnorm(x_ref[...], g_ref[...], sc_ref[...], sh_ref[...], tm).astype(BF16)

    o_ref[...] = jnp.dot(h_ref[...], w_ref[...], preferred_element_type=F32)


def _proj(x, grp, sub, g, w):
    d = x.shape[1]
    ncol = w.shape[1]
    tn = _pick(ncol, (1152, 768, 640, 512, 384, 256, 128))
    tm = grp.tm
    return pl.pallas_call(
        _proj_kernel,
        out_shape=SDS((grp.rows, ncol), F32),
        grid=(grp.rows // tm, ncol // tn),
        in_specs=[BS((tm, d), lambda i, j: (i, 0)),
                  grp.mod_spec(sub, 0), grp.mod_spec(sub, 1),
                  BS((1, d), lambda i, j: (0, 0)),
                  BS((d, tn), lambda i, j: (0, j))],
        out_specs=BS((tm, tn), lambda i, j: (i, j)),
        scratch_shapes=[pltpu.VMEM((tm, d), BF16)],
        compiler_params=_cp(2),
    )(x, grp.mods, grp.mods, g.reshape(1, d), w)


def _out0_kernel(x_ref, gt_ref, oc_ref, os_ref, ow_ref, gz_ref, yb_ref, wa_ref, wb_ref, o_ref):
    tm = x_ref.shape[0]
    gates = jax.nn.sigmoid(gz_ref[...])
    oc, osel, ow = oc_ref[...], os_ref[...], ow_ref[...]
    parts = []
    for h in range(A_HEADS):
        sl = slice(h * A_HD, (h + 1) * A_HD)
        parts.append(gates[:, 3 * h:3 * h + 1] * oc[:, sl] + gates[:, 3 * h + 1:3 * h + 2] * osel[:, sl]
                     + gates[:, 3 * h + 2:3 * h + 3] * ow[:, sl])
    oa = jnp.concatenate(parts, axis=1).astype(BF16)
    y = jnp.dot(oa, wa_ref[...], preferred_element_type=F32)
    y = y + jnp.dot(yb_ref[...].astype(BF16), wb_ref[...], preferred_element_type=F32)
    o_ref[...] = x_ref[...] + _rows(gt_ref[...], tm) * y


def _out0(x, grp, o_c, o_s, o_w, z, y_b, w_a, w_b):
    d = x.shape[1]
    tm = grp.tm
    row = lambda w: BS((tm, w), lambda i: (i, 0))
    return pl.pallas_call(
        _out0_kernel,
        out_shape=SDS(x.shape, F32),
        grid=(grp.rows // tm,),
        in_specs=[row(d), grp.mod_spec(1, 2), row(A_Q), row(A_Q), row(A_Q),
                  BS((tm, LANES), lambda i: (i, L0_GT // LANES)), row(B_CH),
                  BS((A_Q, d), lambda i: (0, 0)), BS((B_CH, d), lambda i: (0, 0))],
        out_specs=row(d),
        compiler_params=_cp(1),
    )(x, grp.mods, o_c, o_s, o_w, z, y_b, w_a, w_b)


def _out1_kernel(x_ref, gt_ref, od_ref, yc_ref, wa_ref, wb_ref, o_ref):
    tm = x_ref.shape[0]
    y = jnp.dot(od_ref[...].astype(BF16), wa_ref[...], preferred_element_type=F32)
    y = y + jnp.dot(yc_ref[...].astype(BF16), wb_ref[...], preferred_element_type=F32)
    o_ref[...] = x_ref[...] + _rows(gt_ref[...], tm) * y


def _out1(x, grp, o_d, y_c, w_a, w_b):
    d = x.shape[1]
    tm = grp.tm
    row = lambda w: BS((tm, w), lambda i: (i, 0))
    return pl.pallas_call(
        _out1_kernel,
        out_shape=SDS(x.shape, F32),
        grid=(grp.rows // tm,),
        in_specs=[row(d), grp.mod_spec(1, 2), row(D_Q), row(C_CH),
                  BS((D_Q, d), lambda i: (0, 0)), BS((C_CH, d), lambda i: (0, 0))],
        out_specs=row(d),
        compiler_params=_cp(1),
    )(x, grp.mods, o_d, y_c, w_a, w_b)


def _rms_kernel(x_ref, g_ref, o_ref):
    x = x_ref[...]
    o_ref[...] = x * lax.rsqrt(jnp.mean(x * x, axis=-1, keepdims=True) + RMS_EPS) * g_ref[...]


def _final_norm(x, g, tm):
    rows, d = x.shape
    return pl.pallas_call(
        _rms_kernel, out_shape=SDS(x.shape, F32), grid=(rows // tm,),
        in_specs=[BS((tm, d), lambda i: (i, 0)), BS((1, d), lambda i: (0, 0))],
        out_specs=BS((tm, d), lambda i: (i, 0)), compiler_params=_cp(1),
    )(x, g.reshape(1, d))


def _stack_heads(q_ref, kv, g_heads, hd, scale):
    qs = [q_ref[:, (kv * g_heads + g) * hd:(kv * g_heads + g + 1) * hd] for g in range(g_heads)]
    return (jnp.concatenate(qs, axis=0) * scale).astype(BF16)


def _unstack_heads(o_ref, o, kv, g_heads, hd, tq):
    for g in range(g_heads):
        c0 = (kv * g_heads + g) * hd
        o_ref[:, c0:c0 + hd] = o[g * tq:(g + 1) * tq, :]


def _masked_softmax(s, valid, sink=None):
    s = jnp.where(valid, s, NEG)
    m = jnp.max(s, axis=1, keepdims=True)
    if sink is not None:
        m = jnp.maximum(m, sink)
    p = jnp.where(valid, jnp.exp(s - m), 0.0)
    den = jnp.sum(p, axis=1, keepdims=True)
    if sink is not None:
        den = den + jnp.exp(sink - m)
    return p, jnp.maximum(den, TINY)


def _band_kernel(q_ref, k_ref, v_ref, *rest, n_kv, g_heads, hd, window, tq, band, t_len, slopes, has_sink):
    if has_sink:
        sink_ref, o_ref = rest
    else:
        (o_ref,) = rest
    t0 = pl.program_id(1) * tq
    k0 = pl.multiple_of(jnp.clip(t0 - window, 0, t_len - band), tq)
    rows = g_heads * tq
    ti = lax.broadcasted_iota(jnp.int32, (rows, band), 0) % tq
    cc = lax.broadcasted_iota(jnp.int32, (rows, band), 1)
    dist = (t0 + ti) - (k0 + cc)
    valid = (dist >= 0) & (dist < window)
    distf = dist.astype(F32)
    for kv in range(n_kv):
        q = _stack_heads(q_ref, kv, g_heads, hd, hd ** -0.5)
        k = k_ref[pl.ds(k0, band), kv * hd:(kv + 1) * hd].astype(BF16)
        v = v_ref[pl.ds(k0, band), kv * hd:(kv + 1) * hd].astype(BF16)
        s = _nt_dot(q, k) - _col(slopes[kv], tq) * distf
        p, den = _masked_softmax(s, valid, sink_ref[kv] if has_sink else None)
        o = jnp.dot(p.astype(BF16), v, preferred_element_type=F32) / den
        _unstack_heads(o_ref, o, kv, g_heads, hd, tq)


def _band_attn(z, b, t, n_kv, g_heads, hd, window, k_off, slopes, sinks=None):
    tq = 128
    band = min(window + tq, t)
    wkv = n_kv * hd
    nq = n_kv * g_heads * hd
    tpb = t // tq
    in_specs = [BS((tq, nq), lambda bi, qi: (bi * tpb + qi, 0)),
                BS((t, wkv), lambda bi, qi: (bi, k_off // wkv)),
                BS((t, wkv), lambda bi, qi: (bi, k_off // wkv + 1))]
    args = [z, z, z]
    if sinks is not None:
        sink_rows = jnp.repeat(sinks.astype(F32).reshape(n_kv, g_heads), tq, axis=1).reshape(n_kv, g_heads * tq, 1)
        in_specs.append(BS((n_kv, g_heads * tq, 1), lambda bi, qi: (0, 0, 0)))
        args.append(sink_rows)
    return pl.pallas_call(
        functools.partial(_band_kernel, n_kv=n_kv, g_heads=g_heads, hd=hd, window=window, tq=tq, band=band,
                          t_len=t, slopes=slopes, has_sink=sinks is not None),
        out_shape=SDS((b * t, nq), F32),
        grid=(b, tpb),
        in_specs=in_specs,
        out_specs=BS((tq, nq), lambda bi, qi: (bi * tpb + qi, 0)),
        compiler_params=_cp(2),
    )(*args)


def _gelu_tanh(x):
    return 0.5 * x * (1.0 + jnp.tanh(0.7978845608028654 * (x + 0.044715 * x * x * x)))


def _cmp_kernel(*refs, n_pages, page_rows, has_new, n_prefetch):
    refs = refs[n_prefetch:]
    pages = refs[:n_pages]
    pos = n_pages
    new_ref = None
    if has_new:
        new_ref = refs[pos]
        pos += 1
    pe_ref, w1_ref, w2_ref, kc_ref, vc_ref, lhs_ref = refs[pos:pos + 6]
    r_real = n_pages * page_rows
    seg = r_real + (16 if has_new else 0)
    hid = A_HD
    width = 2 * A_KV * A_HD
    for t in range(2):
        for kv in range(A_KV):
            s = t * A_KV + kv
            base = kv * seg
            for j in range(CMP_STRIDE):
                c0 = j * width + s * A_HD
                dst = slice(j * A_HD, (j + 1) * A_HD)
                if page_rows % 16 == 0:
                    for p in range(n_pages):
                        lhs_ref[base + p * page_rows:base + (p + 1) * page_rows, dst] = (
                            pages[p][:, c0:c0 + A_HD].astype(BF16))
                else:
                    for p in range(0, n_pages, 2):
                        pair = jnp.concatenate([pages[p][:, c0:c0 + A_HD], pages[p + 1][:, c0:c0 + A_HD]], axis=0)
                        lhs_ref[base + p * page_rows:base + (p + 2) * page_rows, dst] = pair.astype(BF16)
                if has_new:
                    nb = jnp.broadcast_to(new_ref[:, c0:c0 + A_HD], (16, A_HD))
                    lhs_ref[base + r_real:base + seg, dst] = nb.astype(BF16)
        lhs_ref[2 * seg:2 * seg + 16, :] = pe_ref[t].astype(BF16)
        h = jnp.dot(lhs_ref[...], w1_ref[t], preferred_element_type=F32)
        pe_sum = h[2 * seg:2 * seg + 1, 0:hid] + h[2 * seg + 1:2 * seg + 2, hid:2 * hid]
        out_ref = kc_ref if t == 0 else vc_ref
        for kv in range(A_KV):
            first = h[kv * seg:(kv + 1) * seg, 0:hid]
            second = h[kv * seg:(kv + 1) * seg, hid:2 * hid]
            pre = first + pltpu.roll(second, seg - 1, 0) + pe_sum
            o = jnp.dot(_gelu_tanh(pre).astype(BF16), w2_ref[t], preferred_element_type=F32)
            out_ref[:, kv * A_HD:(kv + 1) * A_HD] = o[0:r_real, :]


def _cmp_weights(w_cmp1, pe_cmp, w_cmp2):
    k1 = CMP_STRIDE * A_HD
    w1 = jnp.concatenate([w_cmp1[:, 0].reshape(2, k1, A_HD), w_cmp1[:, 1].reshape(2, k1, A_HD)], axis=2).astype(BF16)
    pe = jnp.pad(pe_cmp.reshape(2, 2, k1), [(0, 0), (0, 14), (0, 0)])
    return w1, pe, w_cmp2.astype(BF16)


def _compress_prompt(kvc, b, t, w1, pe, w2):
    nr = t // CMP_STRIDE
    width = CMP_STRIDE * 2 * A_KV * A_HD
    k1 = CMP_STRIDE * A_HD
    x = kvc.reshape(b * nr, width)
    full = lambda shp: BS(shp, lambda i: (0,) * len(shp))
    return pl.pallas_call(
        functools.partial(_cmp_kernel, n_pages=1, page_rows=nr, has_new=False, n_prefetch=0),
        out_shape=(SDS((b, nr, A_KV * A_HD), F32),) * 2,
        grid=(b,),
        in_specs=[BS((nr, width), lambda i: (i, 0)), full(pe.shape), full(w1.shape), full(w2.shape)],
        out_specs=(BS((None, nr, A_KV * A_HD), lambda i: (i, 0, 0)),) * 2,
        scratch_shapes=[pltpu.VMEM((2 * nr + 16, k1), BF16)],
        compiler_params=_cp(1),
    )(x, pe, w1, w2)


def _compress_sample(cache_cmp, page_table, kvc_new, w1, pe, w2):
    n, n_pages = page_table.shape
    page = cache_cmp.shape[1]
    pr = page // CMP_STRIDE
    width = CMP_STRIDE * 2 * A_KV * A_HD
    k1 = CMP_STRIDE * A_HD
    ts = kvc_new.shape[1]
    pages = cache_cmp.reshape(cache_cmp.shape[0], pr, width)
    new_blk = jnp.pad(kvc_new, [(0, 0), (0, CMP_STRIDE - ts), (0, 0)]).reshape(n, 1, width)
    nr = n_pages * pr
    seg = nr + 16
    full = lambda shp: BS(shp, lambda i, pt: (0,) * len(shp))
    in_specs = [BS((None, pr, width), (lambda i, pt, p=p: (pt[i, p], 0, 0))) for p in range(n_pages)]
    in_specs += [BS((None, 1, width), lambda i, pt: (i, 0, 0)), full(pe.shape), full(w1.shape), full(w2.shape)]
    return pl.pallas_call(
        functools.partial(_cmp_kernel, n_pages=n_pages, page_rows=pr, has_new=True, n_prefetch=1),
        out_shape=(SDS((n, nr, A_KV * A_HD), F32),) * 2,
        grid_spec=pltpu.PrefetchScalarGridSpec(
            num_scalar_prefetch=1, grid=(n,), in_specs=in_specs,
            out_specs=(BS((None, nr, A_KV * A_HD), lambda i, pt: (i, 0, 0)),) * 2,
            scratch_shapes=[pltpu.VMEM((2 * seg + 16, k1), BF16)]),
        compiler_params=_cp(1),
    )(page_table, *([pages] * n_pages), new_blk, pe, w1, w2)


def _cmpsel_p_kernel(q_ref, kc_ref, vc_ref, a_ref, oc_ref, sel_ref, *, tq, nc, ns, slopes):
    t0 = pl.program_id(1) * tq
    rows = A_G * tq
    qpos = t0 + lax.broadcasted_iota(jnp.int32, (rows, nc), 0) % tq
    end = lax.broadcasted_iota(jnp.int32, (rows, nc), 1) * CMP_STRIDE + (2 * CMP_STRIDE - 1)
    dist = qpos - end
    valid = dist >= 0
    distf = dist.astype(F32)
    for kv in range(A_KV):
        q = _stack_heads(q_ref, kv, A_G, A_HD, A_HD ** -0.5)
        kc = kc_ref[:, kv * A_HD:(kv + 1) * A_HD].astype(BF16)
        vc = vc_ref[:, kv * A_HD:(kv + 1) * A_HD].astype(BF16)
        s = _nt_dot(q, kc) - _col(slopes[kv], tq) * distf
        p, den = _masked_softmax(s, valid)
        pn = p / den
        _unstack_heads(oc_ref, jnp.dot(pn.astype(BF16), vc, preferred_element_type=F32), kv, A_G, A_HD, tq)
        psum = pn[0:tq] + pn[tq:2 * tq] + pn[2 * tq:3 * tq] + pn[3 * tq:4 * tq]
        score = _select_scores(_split3_dot(psum, a_ref[...]), t0)
        sel_ref[:, kv * ns:(kv + 1) * ns] = _topk_mask(score, min(SEL_TOPK, ns))


def _cmpsel_prompt(z, k_c, v_c, b, t, slopes):
    tq = 128
    nc = t // CMP_STRIDE
    ns = t // SEL_BLOCK
    tpb = t // tq
    a = _sel_matrix(nc, ns)
    return pl.pallas_call(
        functools.partial(_cmpsel_p_kernel, tq=tq, nc=nc, ns=ns, slopes=slopes),
        out_shape=(SDS((b * t, A_Q), F32), SDS((b * t, A_KV * ns), F32)),
        grid=(b, tpb),
        in_specs=[BS((tq, A_Q), lambda bi, qi: (bi * tpb + qi, 0)),
                  BS((None, nc, A_KV * A_HD), lambda bi, qi: (bi, 0, 0)),
                  BS((None, nc, A_KV * A_HD), lambda bi, qi: (bi, 0, 0)),
                  BS((nc, ns), lambda bi, qi: (0, 0))],
        out_specs=(BS((tq, A_Q), lambda bi, qi: (bi * tpb + qi, 0)),
                   BS((tq, A_KV * ns), lambda bi, qi: (bi * tpb + qi, 0))),
        compiler_params=_cp(2),
    )(z, k_c, v_c, a)


def _sel_p_kernel(q_ref, k_ref, v_ref, sel_ref, e_ref, o_ref, m_ref, l_ref, acc_ref, *, tq, tk, ns, slopes):
    t0 = pl.program_id(1) * tq
    rows = A_G * tq
    n_tiles = (t0 + tq - 1) // tk + 1
    qpos = t0 + lax.broadcasted_iota(jnp.int32, (rows, tk), 0) % tq
    cc = lax.broadcasted_iota(jnp.int32, (rows, tk), 1)
    for kv in range(A_KV):
        q = _stack_heads(q_ref, kv, A_G, A_HD, A_HD ** -0.5)
        selk = sel_ref[:, kv * ns:(kv + 1) * ns].astype(BF16)
        slope = _col(slopes[kv], tq)
        m_ref[...] = jnp.full(m_ref.shape, NEG, F32)
        l_ref[...] = jnp.zeros(l_ref.shape, F32)
        acc_ref[...] = jnp.zeros(acc_ref.shape, F32)

        def body(c, carry):
            ks = pl.multiple_of(c * tk, tk)
            k = k_ref[pl.ds(ks, tk), kv * A_HD:(kv + 1) * A_HD].astype(BF16)
            v = v_ref[pl.ds(ks, tk), kv * A_HD:(kv + 1) * A_HD].astype(BF16)
            dist = qpos - (ks + cc)
            mk = jnp.dot(selk, e_ref[c], preferred_element_type=F32)
            valid = (jnp.concatenate([mk] * A_G, axis=0) > 0.5) & (dist >= 0)
            s = jnp.where(valid, _nt_dot(q, k) - slope * dist.astype(F32), NEG)
            m_old = m_ref[...]
            m_new = jnp.maximum(m_old, jnp.max(s, axis=1, keepdims=True))
            alpha = jnp.exp(m_old - m_new)
            p = jnp.where(valid, jnp.exp(s - m_new), 0.0)
            l_ref[...] = alpha * l_ref[...] + jnp.sum(p, axis=1, keepdims=True)
            acc_ref[...] = alpha * acc_ref[...] + jnp.dot(p.astype(BF16), v, preferred_element_type=F32)
            m_ref[...] = m_new
            return carry

        lax.fori_loop(0, n_tiles, body, 0)
        o = acc_ref[...] / jnp.maximum(l_ref[...], TINY)
        _unstack_heads(o_ref, o, kv, A_G, A_HD, tq)


def _sel_prompt(z, sel, b, t, slopes):
    tq = 128
    tk = min(256, t)
    ns = t // SEL_BLOCK
    tpb = t // tq
    wkv = A_KV * A_HD
    e = np.zeros((t // tk, ns, tk), np.float32)
    for c in range(t // tk):
        for x in range(tk):
            e[c, (c * tk + x) // SEL_BLOCK, x] = 1.0
    rows = A_G * tq
    return pl.pallas_call(
        functools.partial(_sel_p_kernel, tq=tq, tk=tk, ns=ns, slopes=slopes),
        out_shape=SDS((b * t, A_Q), F32),
        grid=(b, tpb),
        in_specs=[BS((tq, A_Q), lambda bi, qi: (bi * tpb + qi, 0)),
                  BS((t, wkv), lambda bi, qi: (bi, L0_KVS // wkv)),
                  BS((t, wkv), lambda bi, qi: (bi, L0_KVS // wkv + 1)),
                  BS((tq, A_KV * ns), lambda bi, qi: (bi * tpb + qi, 0)),
                  BS((t // tk, ns, tk), lambda bi, qi: (0, 0, 0))],
        out_specs=BS((tq, A_Q), lambda bi, qi: (bi * tpb + qi, 0)),
        scratch_shapes=[pltpu.VMEM((rows, 1), F32), pltpu.VMEM((rows, 1), F32), pltpu.VMEM((rows, A_HD), F32)],
        compiler_params=_cp(2),
    )(z, z, z, sel, jnp.asarray(e, BF16))


def _conv_p_kernel(u_ref, b_ref, c_ref, w_ref, y_ref, tail_ref, carry_ref):
    tr = u_ref.shape[0]

    @pl.when(pl.program_id(1) == 0)
    def _():
        carry_ref[...] = jnp.zeros_like(carry_ref)

    v = c_ref[...] * u_ref[...]
    ext = jnp.concatenate([carry_ref[...], v], axis=0)
    v1 = pltpu.roll(ext, 1, 0)[8:]
    v2 = pltpu.roll(ext, 2, 0)[8:]
    y_ref[...] = b_ref[...] * (w_ref[0:1, :] * v2 + w_ref[1:2, :] * v1 + w_ref[2:3, :] * v)
    carry_ref[...] = v[tr - 8:]
    tail_ref[...] = v[tr - 8:]


def _conv_prompt(z, b, t, w_conv):
    tr = min(512, t)
    tpb = t // tr
    blk = lambda off: BS((tr, B_CH), lambda bi, j: (bi * tpb + j, off // B_CH))
    return pl.pallas_call(
        _conv_p_kernel,
        out_shape=(SDS((b * t, B_CH), F32), SDS((b, 8, B_CH), F32)),
        grid=(b, tpb),
        in_specs=[blk(L0_U), blk(L0_B), blk(L0_C), BS((CONV_W, B_CH), lambda bi, j: (0, 0))],
        out_specs=(BS((tr, B_CH), lambda bi, j: (bi * tpb + j, 0)), BS((None, 8, B_CH), lambda bi, j: (bi, 0, 0))),
        scratch_shapes=[pltpu.VMEM((8, B_CH), F32)],
        compiler_params=_cp(2),
    )(z, z, z, w_conv)


def _pool_p_kernel(u_ref, wp_ref, ps_ref, y_ref, tail_ref, carry_ref):
    tr = u_ref.shape[0]
    hist = carry_ref.shape[0]

    @pl.when(pl.program_id(1) == 0)
    def _():
        carry_ref[...] = jnp.zeros_like(carry_ref)

    u = u_ref[...]
    ext = jnp.concatenate([carry_ref[...], u], axis=0)
    pos = pl.program_id(1) * tr + lax.broadcasted_iota(jnp.int32, (tr, 1), 0)
    for g, w in enumerate(POOL_WINDOWS):
        cs = slice(g * C_GRP, (g + 1) * C_GRP)
        s = ext[:, cs]
        sh = 1
        while sh < w:
            s = s + pltpu.roll(s, sh, 0)
            sh *= 2
        cnt = jnp.minimum(w, pos + 1).astype(F32)
        dlt = s[hist:] / cnt - u[:, cs]
        y_ref[:, cs] = jnp.dot(dlt.astype(BF16), wp_ref[g], preferred_element_type=F32) * ps_ref[:, cs]
    carry_ref[...] = u[tr - hist:]
    tail_ref[...] = u[tr - hist:]


def _pool_prompt(z, b, t, w_pool, pool_scale):
    tr = min(512, t)
    tpb = t // tr
    hist = 16
    return pl.pallas_call(
        _pool_p_kernel,
        out_shape=(SDS((b * t, C_CH), F32), SDS((b, hist, C_CH), F32)),
        grid=(b, tpb),
        in_specs=[BS((tr, C_CH), lambda bi, j: (bi * tpb + j, L1_U // C_CH)),
                  BS(w_pool.shape, lambda bi, j: (0, 0, 0)),
                  BS((1, C_CH), lambda bi, j: (0, 0))],
        out_specs=(BS((tr, C_CH), lambda bi, j: (bi * tpb + j, 0)), BS((None, hist, C_CH), lambda bi, j: (bi, 0, 0))),
        scratch_shapes=[pltpu.VMEM((hist, C_CH), F32)],
        compiler_params=_cp(2),
    )(z, w_pool, pool_scale.reshape(1, C_CH))


def _conv_s_kernel(u_ref, b_ref, c_ref, st_ref, w_ref, y_ref, ns_ref):
    ts = u_ref.shape[0]
    ext = [st_ref[i] for i in range(CONV_W - 1)] + [c_ref[t] * u_ref[t] for t in range(ts)]
    for t in range(ts):
        y_ref[t] = b_ref[t] * (w_ref[0:1, :] * ext[t] + w_ref[1:2, :] * ext[t + 1] + w_ref[2:3, :] * ext[t + 2])
    for i in range(CONV_W - 1):
        ns_ref[i] = ext[ts + i]


def _conv_sample(z3, state_t, w_conv):
    ts, n, _ = z3.shape
    cb = 256
    blk = lambda off: BS((ts, n, cb), lambda j: (0, 0, off // cb + j))
    return pl.pallas_call(
        _conv_s_kernel,
        out_shape=(SDS((ts, n, B_CH), F32), SDS((CONV_W - 1, n, B_CH), F32)),
        grid=(B_CH // cb,),
        in_specs=[blk(L0_U), blk(L0_B), blk(L0_C), BS((CONV_W - 1, n, cb), lambda j: (0, 0, j)),
                  BS((CONV_W, cb), lambda j: (0, j))],
        out_specs=(BS((ts, n, cb), lambda j: (0, 0, j)), BS((CONV_W - 1, n, cb), lambda j: (0, 0, j))),
        compiler_params=_cp(1),
    )(z3, z3, z3, state_t, w_conv)


def _pool_s_kernel(u_ref, st_ref, wp_ref, ps_ref, y_ref, *, past):
    ts, n, _ = u_ref.shape
    hist = st_ref.shape[0]
    for g, w in enumerate(POOL_WINDOWS):
        @pl.when(pl.program_id(0) == g)
        def _(w=w):
            ext = [st_ref[e] for e in range(hist)] + [u_ref[t] for t in range(ts)]
            ds = []
            for t in range(ts):
                acc = ext[hist + t]
                for i in range(1, w):
                    acc = acc + ext[hist + t - i]
                ds.append(acc / float(min(w, past + t + 1)) - ext[hist + t])
            y = jnp.dot(jnp.concatenate(ds, axis=0).astype(BF16), wp_ref[...], preferred_element_type=F32) * ps_ref[...]
            for t in range(ts):
                y_ref[t] = y[t * n:(t + 1) * n]


def _pool_sample(z3, state_t, w_pool, pool_scale, past):
    ts, n, _ = z3.shape
    hist = state_t.shape[0]
    ng = len(POOL_WINDOWS)
    return pl.pallas_call(
        functools.partial(_pool_s_kernel, past=past),
        out_shape=SDS((ts, n, C_CH), F32),
        grid=(ng,),
        in_specs=[BS((ts, n, C_GRP), lambda g: (0, 0, L1_U // C_GRP + g)),
                  BS((hist, n, C_GRP), lambda g: (0, 0, g)),
                  BS((None, C_GRP, C_GRP), lambda g: (g, 0, 0)),
                  BS((1, C_GRP), lambda g: (0, g))],
        out_specs=BS((ts, n, C_GRP), lambda g: (0, 0, g)),
        compiler_params=_cp(1),
    )(z3, state_t, w_pool, pool_scale.reshape(1, C_CH))


def _win_s_kernel(q_ref, st_ref, kn_ref, *rest, n_kv, g_heads, hd, win, ts, slopes, has_sink):
    if has_sink:
        sink_ref, o_ref = rest
    else:
        (o_ref,) = rest
    rows = g_heads * ts
    nk = win + LANES
    ti = lax.broadcasted_iota(jnp.int32, (rows, nk), 0) % ts
    dist = ti + win - lax.broadcasted_iota(jnp.int32, (rows, nk), 1)
    valid = (dist >= 0) & (dist < win)
    distf = dist.astype(F32)
    zpad = jnp.zeros((LANES - ts, hd), F32)
    for kv in range(n_kv):
        q = _stack_heads(q_ref, kv, g_heads, hd, hd ** -0.5)
        kc = slice(kv * hd, (kv + 1) * hd)
        vc = slice((n_kv + kv) * hd, (n_kv + kv + 1) * hd)
        k = jnp.concatenate([st_ref[:, kc], kn_ref[:, kc], zpad], axis=0).astype(BF16)
        v = jnp.concatenate([st_ref[:, vc], kn_ref[:, vc], zpad], axis=0).astype(BF16)
        s = _nt_dot(q, k) - _col(slopes[kv], ts) * distf
        p, den = _masked_softmax(s, valid, sink_ref[kv] if has_sink else None)
        o = jnp.dot(p.astype(BF16), v, preferred_element_type=F32) / den
        _unstack_heads(o_ref, o, kv, g_heads, hd, ts)


def _win_sample(q_nm, state, kv_new_nm, n, ts, n_kv, g_heads, hd, slopes, sinks=None):
    win = state.shape[1]
    nq = n_kv * g_heads * hd
    wkv = 2 * n_kv * hd
    in_specs = [BS((ts, nq), lambda i: (i, 0)), BS((None, win, wkv), lambda i: (i, 0, 0)), BS((ts, wkv), lambda i: (i, 0))]
    args = [q_nm, state, kv_new_nm]
    if sinks is not None:
        sink_rows = jnp.repeat(sinks.astype(F32).reshape(n_kv, g_heads), ts, axis=1).reshape(n_kv, g_heads * ts, 1)
        in_specs.append(BS((n_kv, g_heads * ts, 1), lambda i: (0, 0, 0)))
        args.append(sink_rows)
    return pl.pallas_call(
        functools.partial(_win_s_kernel, n_kv=n_kv, g_heads=g_heads, hd=hd, win=win, ts=ts, slopes=slopes,
                          has_sink=sinks is not None),
        out_shape=SDS((n * ts, nq), F32),
        grid=(n,),
        in_specs=in_specs,
        out_specs=BS((ts, nq), lambda i: (i, 0)),
        compiler_params=_cp(1),
    )(*args)


def _cmpsel_s_kernel(q_ref, kc_ref, vc_ref, a_ref, oc_ref, sel_ref, *, ts, nc, n_past, past, slopes):
    rows = A_G * ts
    qpos = past + lax.broadcasted_iota(jnp.int32, (rows, nc), 0) % ts
    end = lax.broadcasted_iota(jnp.int32, (rows, nc), 1) * CMP_STRIDE + (2 * CMP_STRIDE - 1)
    dist = qpos - end
    valid = dist >= 0
    distf = dist.astype(F32)
    ns_pad = a_ref.shape[1]
    lane_ok = lax.broadcasted_iota(jnp.int32, (ts, ns_pad), 1) <= n_past
    for kv in range(A_KV):
        q = _stack_heads(q_ref, kv, A_G, A_HD, A_HD ** -0.5)
        kc = kc_ref[:, kv * A_HD:(kv + 1) * A_HD].astype(BF16)
        vc = vc_ref[:, kv * A_HD:(kv + 1) * A_HD].astype(BF16)
        s = _nt_dot(q, kc) - _col(slopes[kv], ts) * distf
        p, den = _masked_softmax(s, valid)
        pn = p / den
        _unstack_heads(oc_ref, jnp.dot(pn.astype(BF16), vc, preferred_element_type=F32), kv, A_G, A_HD, ts)
        psum = pn[0:ts] + pn[ts:2 * ts] + pn[2 * ts:3 * ts] + pn[3 * ts:4 * ts]
        score = _select_scores(_split3_dot(psum, a_ref[...]), past)
        score = jnp.where(lane_ok, score, -2.0)
        sel = _topk_mask(score, SEL_TOPK)
        sel_ref[:, kv * n_past:(kv + 1) * n_past] = sel[:, 0:n_past]


def _cmpsel_sample(q_nm, k_c, v_c, n, ts, past, slopes):
    nc = k_c.shape[1]
    n_past = past // SEL_BLOCK
    ns_pad = -(-(n_past + 1) // LANES) * LANES
    a = _sel_matrix(nc, ns_pad)
    return pl.pallas_call(
        functools.partial(_cmpsel_s_kernel, ts=ts, nc=nc, n_past=n_past, past=past, slopes=slopes),
        out_shape=(SDS((n * ts, A_Q), F32), SDS((n, ts, A_KV * n_past), F32)),
        grid=(n,),
        in_specs=[BS((ts, A_Q), lambda i: (i, 0)),
                  BS((None, nc, A_KV * A_HD), lambda i: (i, 0, 0)),
                  BS((None, nc, A_KV * A_HD), lambda i: (i, 0, 0)),
                  BS((nc, ns_pad), lambda i: (0, 0))],
        out_specs=(BS((ts, A_Q), lambda i: (i, 0)), BS((None, ts, A_KV * n_past), lambda i: (i, 0, 0))),
        compiler_params=_cp(1),
    )(q_nm, k_c, v_c, a)


def _sel_s_kernel(pt_ref, q_ref, sel_ref, kn_ref, e_ref, *rest, n_pages, page, ts, n_past, past, slopes):
    pages = rest[:n_pages]
    o_ref, kv_ref = rest[n_pages:]
    for p in range(n_pages):
        kv_ref[p * page:(p + 1) * page, :] = pages[p][...].astype(BF16)
    rows = A_G * ts
    ti_p = lax.broadcasted_iota(jnp.int32, (rows, past), 0) % ts
    dist_p = (past + ti_p - lax.broadcasted_iota(jnp.int32, (rows, past), 1)).astype(F32)
    ti_n = lax.broadcasted_iota(jnp.int32, (rows, LANES), 0) % ts
    dist_n = ti_n - lax.broadcasted_iota(jnp.int32, (rows, LANES), 1)
    valid_n = dist_n >= 0
    zpad = jnp.zeros((LANES - ts, A_HD), F32)
    for kv in range(A_KV):
        q = _stack_heads(q_ref, kv, A_G, A_HD, A_HD ** -0.5)
        slope = _col(slopes[kv], ts)
        kc = slice(kv * A_HD, (kv + 1) * A_HD)
        vc = slice((A_KV + kv) * A_HD, (A_KV + kv + 1) * A_HD)
        selk = sel_ref[:, kv * n_past:(kv + 1) * n_past].astype(BF16)
        mk = jnp.dot(jnp.concatenate([selk] * A_G, axis=0), e_ref[...], preferred_element_type=F32)
        valid_p = mk > 0.5
        s_p = jnp.where(valid_p, _nt_dot(q, kv_ref[:, kc]) - slope * dist_p, NEG)
        k_n = jnp.concatenate([kn_ref[:, kc], zpad], axis=0).astype(BF16)
        v_n = jnp.concatenate([kn_ref[:, vc], zpad], axis=0).astype(BF16)
        s_n = jnp.where(valid_n, _nt_dot(q, k_n) - slope * dist_n.astype(F32), NEG)
        m = jnp.maximum(jnp.max(s_p, axis=1, keepdims=True), jnp.max(s_n, axis=1, keepdims=True))
        p_p = jnp.where(valid_p, jnp.exp(s_p - m), 0.0)
        p_n = jnp.where(valid_n, jnp.exp(s_n - m), 0.0)
        den = jnp.sum(p_p, axis=1, keepdims=True) + jnp.sum(p_n, axis=1, keepdims=True)
        o = jnp.dot(p_p.astype(BF16), kv_ref[:, vc], preferred_element_type=F32)
        o = o + jnp.dot(p_n.astype(BF16), v_n, preferred_element_type=F32)
        _unstack_heads(o_ref, o / jnp.maximum(den, TINY), kv, A_G, A_HD, ts)


def _sel_sample(q_nm, sel, kvs_new_nm, cache_sel, page_table, n, ts, past, slopes):
    n_pages = page_table.shape[1]
    page = cache_sel.shape[1]
    wkv = 2 * A_KV * A_HD
    n_past = past // SEL_BLOCK
    pages = cache_sel.reshape(cache_sel.shape[0], page, wkv)
    e = np.zeros((n_past, past), np.float32)
    e[np.arange(past) // SEL_BLOCK, np.arange(past)] = 1.0
    in_specs = [BS((ts, A_Q), lambda i, pt: (i, 0)),
                BS((None, ts, A_KV * n_past), lambda i, pt: (i, 0, 0)),
                BS((ts, wkv), lambda i, pt: (i, 0)),
                BS((n_past, past), lambda i, pt: (0, 0))]
    in_specs += [BS((None, page, wkv), (lambda i, pt, p=p: (pt[i, p], 0, 0))) for p in range(n_pages)]
    return pl.pallas_call(
        functools.partial(_sel_s_kernel, n_pages=n_pages, page=page, ts=ts, n_past=n_past, past=past, slopes=slopes),
        out_shape=SDS((n * ts, A_Q), F32),
        grid_spec=pltpu.PrefetchScalarGridSpec(
            num_scalar_prefetch=1, grid=(n,), in_specs=in_specs,
            out_specs=BS((ts, A_Q), lambda i, pt: (i, 0)),
            scratch_shapes=[pltpu.VMEM((past, wkv), BF16)]),
        compiler_params=_cp(1),
    )(page_table, q_nm, sel, kvs_new_nm, jnp.asarray(e, BF16), *([pages] * n_pages))


def _to_nm(a_tm, n, ts):
    w = a_tm.shape[1]
    return a_tm.reshape(ts, n, w).transpose(1, 0, 2).reshape(n * ts, w)


def _to_tm(a_nm, n, ts):
    w = a_nm.shape[1]
    return a_nm.reshape(n, ts, w).transpose(1, 0, 2).reshape(ts * n, w)


def kernel(x_prompt, x_sample, cache_cmp_kv, cache_sel_kv, state_win_kv, state_conv, state_pool, state_swa_kv, page_table, c_prompt, c_sample, norm_g, w_ada, b_ada, w_ffn_in, w_ffn_out, final_g, w_in0, w_out0, w_cmp1, pe_cmp, w_cmp2, w_conv, w_in1, w_out1, attn_sinks, w_pool, pool_scale):
    b, t, d = x_prompt.shape
    n, ts, _ = x_sample.shape
    page = cache_cmp_kv.shape[1]
    past = page_table.shape[1] * page
    assert t % 512 == 0 or t in (128, 256), t
    assert past % SEL_BLOCK == 0 and ts <= CMP_STRIDE and past >= POOL_STATE and page % (2 * CMP_STRIDE) == 0
    assert page_table.shape[1] % 2 == 0 and ts % 8 == 0 and n % 8 == 0

    slopes_a = _slopes(A_HEADS, A_KV)
    slopes_d = _slopes(D_HEADS, D_KV)

    w0 = jnp.concatenate([w_in0[:, :A_Q], w_in0[:, A_Q + 3 * A_KVW + A_GATES:], w_in0[:, A_Q:A_Q + 3 * A_KVW],
                          w_in0[:, A_Q + 3 * A_KVW:A_Q + 3 * A_KVW + A_GATES],
                          jnp.zeros((d, LANES - A_GATES), F32)], axis=1).astype(BF16)
    w1 = jnp.concatenate([w_in1[:, :D_Q], w_in1[:, D_Q + 2 * D_KVW:], w_in1[:, D_Q:D_Q + 2 * D_KVW]], axis=1).astype(BF16)
    wo0a, wo0b = w_out0[:A_Q].astype(BF16), w_out0[A_Q:].astype(BF16)
    wo1a, wo1b = w_out1[:D_Q].astype(BF16), w_out1[D_Q:].astype(BF16)
    wf_in, wf_out = w_ffn_in.astype(BF16), w_ffn_out.astype(BF16)
    wc1, pe, wc2 = _cmp_weights(w_cmp1, pe_cmp, w_cmp2)
    wp = w_pool.astype(BF16)

    m_all = -(-(n + b) // 8) * 8
    c_all = jnp.concatenate([c_sample, c_prompt, jnp.zeros((m_all - n - b, d), F32)], axis=0)
    mods = _ada(c_all, w_ada, b_ada)

    tm_p = min(512, t)
    rows_s = n * ts
    tm_s = rows_s if rows_s <= 512 else 512
    assert tm_s % n == 0 and rows_s % tm_s == 0

    xp = x_prompt.reshape(b * t, d)
    xs = x_sample.transpose(1, 0, 2).reshape(rows_s, d)

    gp = _prompt_group(mods[0, n:n + b], b, t, d, tm_p)
    gs = _sample_group(mods[0, :n], n, ts, d, tm_s)
    xp = _ffn(xp, gp, 0, norm_g[0, 0], wf_in[0, 0], wf_out[0, 0])
    xs = _ffn(xs, gs, 0, norm_g[0, 0], wf_in[0, 0], wf_out[0, 0])
    zp = _proj(xp, gp, 1, norm_g[0, 1], w0)
    zs = _proj(xs, gs, 1, norm_g[0, 1], w0)

    cmp_p = zp[:, L0_KVC:L0_KVC + A_KVW]
    sel_p = zp[:, L0_KVS:L0_KVS + A_KVW]
    win_p = zp[:, L0_KVW:L0_KVW + A_KVW]
    kc_p, vc_p = _compress_prompt(cmp_p, b, t, wc1, pe, wc2)
    oc_p, selmask_p = _cmpsel_prompt(zp, kc_p, vc_p, b, t, slopes_a)
    os_p = _sel_prompt(zp, selmask_p, b, t, slopes_a)
    ow_p = _band_attn(zp, b, t, A_KV, A_G, A_HD, A_WIN, L0_KVW, slopes_a)
    yb_p, vtail_p = _conv_prompt(zp, b, t, w_conv)
    xp = _out0(xp, gp, oc_p, os_p, ow_p, zp, yb_p, wo0a, wo0b)

    q_nm = _to_nm(zs[:, :A_Q], n, ts)
    cmp_s = _to_nm(zs[:, L0_KVC:L0_KVC + A_KVW], n, ts)
    sel_s = _to_nm(zs[:, L0_KVS:L0_KVS + A_KVW], n, ts)
    win_s = _to_nm(zs[:, L0_KVW:L0_KVW + A_KVW], n, ts)
    kc_s, vc_s = _compress_sample(cache_cmp_kv, page_table, cmp_s.reshape(n, ts, A_KVW), wc1, pe, wc2)
    oc_s, selmask_s = _cmpsel_sample(q_nm, kc_s, vc_s, n, ts, past, slopes_a)
    os_s = _sel_sample(q_nm, selmask_s, sel_s, cache_sel_kv, page_table, n, ts, past, slopes_a)
    ow_s = _win_sample(q_nm, state_win_kv.reshape(n, A_WIN, A_KVW), win_s, n, ts, A_KV, A_G, A_HD, slopes_a)
    zs3 = zs.reshape(ts, n, L0_N)
    yb_s, conv_state_t = _conv_sample(zs3, state_conv.transpose(1, 0, 2), w_conv)
    xs = _out0(xs, gs, _to_tm(oc_s, n, ts), _to_tm(os_s, n, ts), _to_tm(ow_s, n, ts), zs,
               yb_s.reshape(rows_s, B_CH), wo0a, wo0b)

    xp = _ffn(xp, gp, 2, norm_g[0, 2], wf_in[0, 1], wf_out[0, 1])
    xs = _ffn(xs, gs, 2, norm_g[0, 2], wf_in[0, 1], wf_out[0, 1])

    gp = _prompt_group(mods[1, n:n + b], b, t, d, tm_p)
    gs = _sample_group(mods[1, :n], n, ts, d, tm_s)
    xp = _ffn(xp, gp, 0, norm_g[1, 0], wf_in[1, 0], wf_out[1, 0])
    xs = _ffn(xs, gs, 0, norm_g[1, 0], wf_in[1, 0], wf_out[1, 0])
    z1p = _proj(xp, gp, 1, norm_g[1, 1], w1)
    z1s = _proj(xs, gs, 1, norm_g[1, 1], w1)

    od_p = _band_attn(z1p, b, t, D_KV, D_G, D_HD, D_WIN, L1_KV, slopes_d, attn_sinks)
    yc_p, utail_p = _pool_prompt(z1p, b, t, wp, pool_scale)
    xp = _out1(xp, gp, od_p, yc_p, wo1a, wo1b)

    q1_nm = _to_nm(z1s[:, :D_Q], n, ts)
    kv1_nm = _to_nm(z1s[:, L1_KV:L1_KV + 2 * D_KVW], n, ts)
    u1_nm = _to_nm(z1s[:, L1_U:L1_U + C_CH], n, ts)
    od_s = _win_sample(q1_nm, state_swa_kv.reshape(n, D_WIN, 2 * D_KVW), kv1_nm, n, ts, D_KV, D_G, D_HD, slopes_d,
                       attn_sinks)
    yc_s = _pool_sample(z1s.reshape(ts, n, L1_N), state_pool.transpose(1, 0, 2), wp, pool_scale, past)
    xs = _out1(xs, gs, _to_tm(od_s, n, ts), yc_s.reshape(rows_s, C_CH), wo1a, wo1b)

    xp = _ffn(xp, gp, 2, norm_g[1, 2], wf_in[1, 1], wf_out[1, 1])
    xs = _ffn(xs, gs, 2, norm_g[1, 2], wf_in[1, 1], wf_out[1, 1])

    y_prompt = _final_norm(xp, final_g, tm_p).reshape(b, t, d)
    y_sample = _final_norm(xs, final_g, tm_s).reshape(ts, n, d).transpose(1, 0, 2)

    kv5 = (2, A_KV, A_HD)
    cmp_kv_prompt = cmp_p.reshape(b, t, *kv5)
    sel_kv_prompt = sel_p.reshape(b, t, *kv5)
    win_kv_prompt = win_p.reshape(b, t, *kv5)[:, t - A_WIN:]
    cmp_kv_sample = cmp_s.reshape(n, ts, *kv5)
    sel_kv_sample = sel_s.reshape(n, ts, *kv5)
    win_kv_sample = jnp.concatenate([state_win_kv, win_s.reshape(n, ts, *kv5)], axis=1)[:, ts:]
    conv_prompt = vtail_p[:, 8 - (CONV_W - 1):]
    conv_sample = conv_state_t.transpose(1, 0, 2)
    pool_prompt = utail_p[:, 16 - POOL_STATE:]
    pool_sample = jnp.concatenate([state_pool, u1_nm.reshape(n, ts, C_CH)], axis=1)[:, ts:]
    swa5 = (2, D_KV, D_HD)
    swa_kv_prompt = z1p[:, L1_KV:L1_KV + 2 * D_KVW].reshape(b, t, *swa5)[:, t - D_WIN:]
    swa_kv_sample = jnp.concatenate([state_swa_kv, kv1_nm.reshape(n, ts, *swa5)], axis=1)[:, ts:]
    return (y_prompt, y_sample, cmp_kv_prompt, cmp_kv_sample, sel_kv_prompt, sel_kv_sample, win_kv_prompt,
            win_kv_sample, conv_prompt, conv_sample, pool_prompt, pool_sample, swa_kv_prompt, swa_kv_sample)
```

```python
import functools

import numpy as np
import jax
import jax.numpy as jnp
from jax import lax
from jax.experimental import pallas as pl
from jax.experimental.pallas import tpu as pltpu

F32 = jnp.float32
BF16 = jnp.bfloat16
SDS = jax.ShapeDtypeStruct
BS = pl.BlockSpec

A_HEADS, A_KV, A_HD = 8, 2, 128
A_G = A_HEADS // A_KV
CMP_STRIDE = 16
SEL_BLOCK, SEL_TOPK, SEL_LOCAL = 64, 16, 2
A_WIN = 512
B_CH, CONV_W = 1024, 3
POOL_WINDOWS = (2, 4, 8, 16)
C_CH = 1024
C_GRP = C_CH // len(POOL_WINDOWS)
POOL_STATE = max(POOL_WINDOWS) - 1
D_HEADS, D_KV, D_HD = 16, 2, 64
D_G = D_HEADS // D_KV
D_WIN = 128
N_SUB = 3
RMS_EPS = 1e-6
NEG = -1e30
FORCED = 1e9
TINY = 1e-30
A_Q = A_HEADS * A_HD
A_KVW = 2 * A_KV * A_HD
A_GATES = 3 * A_HEADS
D_Q = D_HEADS * D_HD
D_KVW = D_KV * D_HD

LANES = 128
L0_U, L0_B, L0_C = A_Q, A_Q + B_CH, A_Q + 2 * B_CH
L0_KVC = A_Q + 3 * B_CH
L0_KVS = L0_KVC + A_KVW
L0_KVW = L0_KVS + A_KVW
L0_GT = L0_KVW + A_KVW
L0_N = L0_GT + LANES
L1_U = D_Q
L1_KV = D_Q + C_CH
L1_N = L1_KV + 2 * D_KVW

VMEM_LIMIT = 56 * 2**20


def _cp(n_axes):
    return pltpu.CompilerParams(dimension_semantics=("arbitrary",) * n_axes, vmem_limit_bytes=VMEM_LIMIT)


def _pick(n, cands):
    for c in cands:
        if n % c == 0:
            return c
    return n


def _slopes(n_heads, n_kv):
    s = 2.0 ** (-8.0 * np.arange(1, n_heads + 1) / n_heads)
    return [[float(v) for v in row] for row in s.reshape(n_kv, n_heads // n_kv)]


def _rows(m, tm):
    r = m.shape[0]
    if r == 1 or r == tm:
        return m
    return jnp.concatenate([m] * (tm // r), axis=0)


def _col(vals, reps):
    return jnp.concatenate([jnp.full((reps, 1), v, F32) for v in vals], axis=0)


def _modnorm(x, g, scale, shift, tm):
    xn = x * lax.rsqrt(jnp.mean(x * x, axis=-1, keepdims=True) + RMS_EPS) * g
    return xn * (1.0 + _rows(scale, tm)) + _rows(shift, tm)


def _nt_dot(a, b):
    return lax.dot_general(a, b, (((1,), (1,)), ((), ())), preferred_element_type=F32)


def _split3_dot(p, a):
    p1 = p.astype(BF16)
    r1 = p - p1.astype(F32)
    p2 = r1.astype(BF16)
    p3 = (r1 - p2.astype(F32)).astype(BF16)
    d = lambda t: jnp.dot(t, a, preferred_element_type=F32)
    return d(p1) + d(p2) + d(p3)


def _topk_mask(score, k):
    lane = lax.broadcasted_iota(jnp.int32, score.shape, 1).astype(F32)
    big = float(score.shape[1])

    def body(_, carry):
        sc, sel = carry
        m = jnp.max(sc, axis=1, keepdims=True)
        idx = jnp.min(jnp.where(sc == m, lane, big), axis=1, keepdims=True)
        pick = lane == idx
        return jnp.where(pick, -3e38, sc), jnp.where(pick, 1.0, sel)

    _, sel = lax.fori_loop(0, k, body, (score, jnp.zeros_like(score)))
    return sel


def _select_scores(score, qpos0):
    j = lax.broadcasted_iota(jnp.int32, score.shape, 1)
    cur = (qpos0 + lax.broadcasted_iota(jnp.int32, score.shape, 0)) // SEL_BLOCK
    forced = (j == 0) | ((cur - j >= 0) & (cur - j < SEL_LOCAL))
    return jnp.where(forced, FORCED, jnp.where(j <= cur, score, -1.0))


def _sel_matrix(nc, ns_pad):
    r = SEL_BLOCK // CMP_STRIDE
    a = np.zeros((nc, ns_pad), np.float32)
    for i in range(nc):
        j = i // r
        if i % r < r - 1:
            a[i, j] = 1.0
        else:
            a[i, j] = 0.5
            if j + 1 < ns_pad:
                a[i, j + 1] = 0.5
    return jnp.asarray(a, BF16)


def _ada_kernel(c_ref, w_ref, b_ref, o_ref):
    c = c_ref[...]
    a = (c * jax.nn.sigmoid(c)).astype(BF16)
    o_ref[...] = jnp.dot(a, w_ref[...].astype(BF16), preferred_element_type=F32) + b_ref[...]


def _ada(c_all, w_ada, b_ada):
    nl, d, n9 = w_ada.shape
    m = c_all.shape[0]
    tn = _pick(n9, (1024, 512, 256, 128))
    return pl.pallas_call(
        _ada_kernel,
        out_shape=SDS((nl, m, n9), F32),
        grid=(nl, n9 // tn),
        in_specs=[BS((m, d), lambda l, j: (0, 0)),
                  BS((None, d, tn), lambda l, j: (l, 0, j)),
                  BS((None, 1, tn), lambda l, j: (l, 0, j))],
        out_specs=BS((None, m, tn), lambda l, j: (l, 0, j)),
        compiler_params=_cp(2),
    )(c_all, w_ada, b_ada.reshape(nl, 1, n9))


class _Group:
    def __init__(self, rows, tm, mods, spec_fn):
        self.rows, self.tm, self.mods, self._spec_fn = rows, tm, mods, spec_fn

    def mod_spec(self, sub, k):
        return self._spec_fn(3 * sub + k)


def _prompt_group(mods_b, b, t, d, tm):
    tpb = t // tm
    arr = mods_b.reshape(b * 3 * N_SUB, 1, d)
    return _Group(b * t, tm, arr, lambda idx: BS((None, 1, d), lambda i, *_: ((i // tpb) * (3 * N_SUB) + idx, 0, 0)))


def _sample_group(mods_n, n, ts, d, tm):
    arr = mods_n.reshape(n, 3 * N_SUB, d).transpose(1, 0, 2)
    return _Group(n * ts, tm, arr, lambda idx: BS((None, n, d), lambda i, *_: (idx, 0, 0)))


def _ffn_kernel(x_ref, sh_ref, sc_ref, gt_ref, g_ref, wg_ref, wu_ref, wo_ref, o_ref, h_ref, *, nf):
    f = pl.program_id(1)
    tm = x_ref.shape[0]

    @pl.when(f == 0)
    def _():
        h_ref[...] = _modnorm(x_ref[...], g_ref[...], sc_ref[...], sh_ref[...], tm).astype(BF16)
        o_ref[...] = jnp.zeros_like(o_ref)

    h = h_ref[...]
    a = jnp.dot(h, wg_ref[...], preferred_element_type=F32)
    u = jnp.dot(h, wu_ref[...], preferred_element_type=F32)
    act = (a * jax.nn.sigmoid(a) * u).astype(BF16)
    o_ref[...] += jnp.dot(act, wo_ref[...], preferred_element_type=F32)

    @pl.when(f == nf - 1)
    def _():
        o_ref[...] = x_ref[...] + 0.5 * _rows(gt_ref[...], tm) * o_ref[...]


def _ffn(x, grp, sub, g, w_in, w_out):
    d = x.shape[1]
    dff = w_out.shape[0]
    tf = _pick(dff, (512, 256, 128))
    nf = dff // tf
    tm = grp.tm
    return pl.pallas_call(
        functools.partial(_ffn_kernel, nf=nf),
        out_shape=SDS(x.shape, F32),
        grid=(grp.rows // tm, nf),
        in_specs=[BS((tm, d), lambda i, f: (i, 0)),
                  grp.mod_spec(sub, 0), grp.mod_spec(sub, 1), grp.mod_spec(sub, 2),
                  BS((1, d), lambda i, f: (0, 0)),
                  BS((d, tf), lambda i, f: (0, f)),
                  BS((d, tf), lambda i, f: (0, nf + f)),
                  BS((tf, d), lambda i, f: (f, 0))],
        out_specs=BS((tm, d), lambda i, f: (i, 0)),
        scratch_shapes=[pltpu.VMEM((tm, d), BF16)],
        compiler_params=_cp(2),
    )(x, grp.mods, grp.mods, grp.mods, g.reshape(1, d), w_in, w_in, w_out)


def _proj_kernel(x_ref, sh_ref, sc_ref, g_ref, w_ref, o_ref, h_ref):
    tm = x_ref.shape[0]

    @pl.when(pl.program_id(1) == 0)
    def _():
        h_ref[...] = _modnorm(x_ref[...], g_ref[...], sc_ref[...], sh_ref[...], tm).astype(BF16)

    o_ref[...] = jnp.dot(h_ref[...], w_ref[...], preferred_element_type=F32)


def _proj(x, grp, sub, g, w):
    d = x.shape[1]
    ncol = w.shape[1]
    tn = _pick(ncol, (1152, 768, 640, 512, 384, 256, 128))
    tm = grp.tm
    return pl.pallas_call(
        _proj_kernel,
        out_shape=SDS((grp.rows, ncol), F32),
        grid=(grp.rows // tm, ncol // tn),
        in_specs=[BS((tm, d), lambda i, j: (i, 0)),
                  grp.mod_spec(sub, 0), grp.mod_spec(sub, 1),
                  BS((1, d), lambda i, j: (0, 0)),
                  BS((d, tn), lambda i, j: (0, j))],
        out_specs=BS((tm, tn), lambda i, j: (i, j)),
        scratch_shapes=[pltpu.VMEM((tm, d), BF16)],
        compiler_params=_cp(2),
    )(x, grp.mods, grp.mods, g.reshape(1, d), w)


def _out0_kernel(x_ref, gt_ref, oc_ref, os_ref, ow_ref, gz_ref, yb_ref, wa_ref, wb_ref, o_ref):
    tm = x_ref.shape[0]
    gates = jax.nn.sigmoid(gz_ref[...])
    oc, osel, ow = oc_ref[...], os_ref[...], ow_ref[...]
    parts = []
    for h in range(A_HEADS):
        sl = slice(h * A_HD, (h + 1) * A_HD)
        parts.append(gates[:, 3 * h:3 * h + 1] * oc[:, sl] + gates[:, 3 * h + 1:3 * h + 2] * osel[:, sl]
                     + gates[:, 3 * h + 2:3 * h + 3] * ow[:, sl])
    oa = jnp.concatenate(parts, axis=1).astype(BF16)
    y = jnp.dot(oa, wa_ref[...], preferred_element_type=F32)
    y = y + jnp.dot(yb_ref[...].astype(BF16), wb_ref[...], preferred_element_type=F32)
    o_ref[...] = x_ref[...] + _rows(gt_ref[...], tm) * y


def _out0(x, grp, o_c, o_s, o_w, z, y_b, w_a, w_b):
    d = x.shape[1]
    tm = grp.tm
    row = lambda w: BS((tm, w), lambda i: (i, 0))
    return pl.pallas_call(
        _out0_kernel,
        out_shape=SDS(x.shape, F32),
        grid=(grp.rows // tm,),
        in_specs=[row(d), grp.mod_spec(1, 2), row(A_Q), row(A_Q), row(A_Q),
                  BS((tm, LANES), lambda i: (i, L0_GT // LANES)), row(B_CH),
                  BS((A_Q, d), lambda i: (0, 0)), BS((B_CH, d), lambda i: (0, 0))],
        out_specs=row(d),
        compiler_params=_cp(1),
    )(x, grp.mods, o_c, o_s, o_w, z, y_b, w_a, w_b)


def _out1_kernel(x_ref, gt_ref, od_ref, yc_ref, wa_ref, wb_ref, o_ref):
    tm = x_ref.shape[0]
    y = jnp.dot(od_ref[...].astype(BF16), wa_ref[...], preferred_element_type=F32)
    y = y + jnp.dot(yc_ref[...].astype(BF16), wb_ref[...], preferred_element_type=F32)
    o_ref[...] = x_ref[...] + _rows(gt_ref[...], tm) * y


def _out1(x, grp, o_d, y_c, w_a, w_b):
    d = x.shape[1]
    tm = grp.tm
    row = lambda w: BS((tm, w), lambda i: (i, 0))
    return pl.pallas_call(
        _out1_kernel,
        out_shape=SDS(x.shape, F32),
        grid=(grp.rows // tm,),
        in_specs=[row(d), grp.mod_spec(1, 2), row(D_Q), row(C_CH),
                  BS((D_Q, d), lambda i: (0, 0)), BS((C_CH, d), lambda i: (0, 0))],
        out_specs=row(d),
        compiler_params=_cp(1),
    )(x, grp.mods, o_d, y_c, w_a, w_b)


def _rms_kernel(x_ref, g_ref, o_ref):
    x = x_ref[...]
    o_ref[...] = x * lax.rsqrt(jnp.mean(x * x, axis=-1, keepdims=True) + RMS_EPS) * g_ref[...]


def _final_norm(x, g, tm):
    rows, d = x.shape
    return pl.pallas_call(
        _rms_kernel, out_shape=SDS(x.shape, F32), grid=(rows // tm,),
        in_specs=[BS((tm, d), lambda i: (i, 0)), BS((1, d), lambda i: (0, 0))],
        out_specs=BS((tm, d), lambda i: (i, 0)), compiler_params=_cp(1),
    )(x, g.reshape(1, d))


def _stack_heads(q_ref, kv, g_heads, hd, scale, r0=0, nr=None):
    rs = slice(None) if nr is None else slice(r0, r0 + nr)
    qs = [q_ref[rs, (kv * g_heads + g) * hd:(kv * g_heads + g + 1) * hd] for g in range(g_heads)]
    return (jnp.concatenate(qs, axis=0) * scale).astype(BF16)


def _unstack_heads(o_ref, o, kv, g_heads, hd, tq, r0=0):
    for g in range(g_heads):
        c0 = (kv * g_heads + g) * hd
        o_ref[r0:r0 + tq, c0:c0 + hd] = o[g * tq:(g + 1) * tq, :]


def _masked_softmax(s, valid, sink=None):
    s = jnp.where(valid, s, NEG)
    m = jnp.max(s, axis=1, keepdims=True)
    if sink is not None:
        m = jnp.maximum(m, sink)
    p = jnp.where(valid, jnp.exp(s - m), 0.0)
    den = jnp.sum(p, axis=1, keepdims=True)
    if sink is not None:
        den = den + jnp.exp(sink - m)
    return p, jnp.maximum(den, TINY)


def _band_kernel(q_ref, k_ref, v_ref, *rest, n_kv, g_heads, hd, window, tq, band, t_len, slopes, has_sink):
    if has_sink:
        sink_ref, o_ref = rest
    else:
        (o_ref,) = rest
    t0 = pl.program_id(1) * tq
    k0 = pl.multiple_of(jnp.clip(t0 - window, 0, t_len - band), tq)
    rows = g_heads * tq
    ti = lax.broadcasted_iota(jnp.int32, (rows, band), 0) % tq
    cc = lax.broadcasted_iota(jnp.int32, (rows, band), 1)
    dist = (t0 + ti) - (k0 + cc)
    valid = (dist >= 0) & (dist < window)
    distf = dist.astype(F32)
    for kv in range(n_kv):
        q = _stack_heads(q_ref, kv, g_heads, hd, hd ** -0.5)
        k = k_ref[pl.ds(k0, band), kv * hd:(kv + 1) * hd].astype(BF16)
        v = v_ref[pl.ds(k0, band), kv * hd:(kv + 1) * hd].astype(BF16)
        s = _nt_dot(q, k) - _col(slopes[kv], tq) * distf
        p, den = _masked_softmax(s, valid, sink_ref[kv] if has_sink else None)
        o = jnp.dot(p.astype(BF16), v, preferred_element_type=F32) / den
        _unstack_heads(o_ref, o, kv, g_heads, hd, tq)


def _band_attn(z, b, t, n_kv, g_heads, hd, window, k_off, slopes, sinks=None):
    tq = 128
    band = min(window + tq, t)
    wkv = n_kv * hd
    nq = n_kv * g_heads * hd
    tpb = t // tq
    in_specs = [BS((tq, nq), lambda bi, qi: (bi * tpb + qi, 0)),
                BS((t, wkv), lambda bi, qi: (bi, k_off // wkv)),
                BS((t, wkv), lambda bi, qi: (bi, k_off // wkv + 1))]
    args = [z, z, z]
    if sinks is not None:
        sink_rows = jnp.repeat(sinks.astype(F32).reshape(n_kv, g_heads), tq, axis=1).reshape(n_kv, g_heads * tq, 1)
        in_specs.append(BS((n_kv, g_heads * tq, 1), lambda bi, qi: (0, 0, 0)))
        args.append(sink_rows)
    return pl.pallas_call(
        functools.partial(_band_kernel, n_kv=n_kv, g_heads=g_heads, hd=hd, window=window, tq=tq, band=band,
                          t_len=t, slopes=slopes, has_sink=sinks is not None),
        out_shape=SDS((b * t, nq), F32),
        grid=(b, tpb),
        in_specs=in_specs,
        out_specs=BS((tq, nq), lambda bi, qi: (bi * tpb + qi, 0)),
        compiler_params=_cp(2),
    )(*args)


def _gelu_tanh(x):
    return 0.5 * x * (1.0 + jnp.tanh(0.7978845608028654 * (x + 0.044715 * x * x * x)))


def _cmp_kernel(*refs, n_pages, page_rows, has_new, n_prefetch, tall):
    refs = refs[n_prefetch:]
    pages = refs[:n_pages]
    pos = n_pages
    new_ref = None
    if has_new:
        new_ref = refs[pos]
        pos += 1
    if tall:
        g_ref = refs[pos]
        pos += 1
    pe_ref, w1_ref, w2_ref, kc_ref, vc_ref, lhs_ref = refs[pos:pos + 6]
    r_real = n_pages * page_rows
    seg = r_real + (16 if has_new else 0)
    hid = A_HD
    width = 2 * A_KV * A_HD
    n_slab = 2 * A_KV

    def regroup(p, s):
        xs = pages[p][pl.ds(s, CMP_STRIDE * page_rows, stride=n_slab), :].astype(BF16)
        return jnp.dot(g_ref[...], xs, preferred_element_type=F32)

    for t in range(2):
        for kv in range(A_KV):
            s = t * A_KV + kv
            base = kv * seg
            if tall:
                for p in range(0, n_pages, 2):
                    pa, pb = regroup(p, s), regroup(p + 1, s)
                    for j in range(CMP_STRIDE):
                        rs = slice(j * page_rows, (j + 1) * page_rows)
                        pair = jnp.concatenate([pa[rs], pb[rs]], axis=0)
                        lhs_ref[base + p * page_rows:base + (p + 2) * page_rows, j * A_HD:(j + 1) * A_HD] = (
                            pair.astype(BF16))
            else:
                for j in range(CMP_STRIDE):
                    c0 = j * width + s * A_HD
                    for p in range(n_pages):
                        lhs_ref[base + p * page_rows:base + (p + 1) * page_rows, j * A_HD:(j + 1) * A_HD] = (
                            pages[p][:, c0:c0 + A_HD].astype(BF16))
            if has_new:
                for j in range(CMP_STRIDE):
                    c0 = j * width + s * A_HD
                    nb = jnp.broadcast_to(new_ref[:, c0:c0 + A_HD], (16, A_HD))
                    lhs_ref[base + r_real:base + seg, j * A_HD:(j + 1) * A_HD] = nb.astype(BF16)
        lhs_ref[2 * seg:2 * seg + 16, :] = pe_ref[t].astype(BF16)
        h = jnp.dot(lhs_ref[...], w1_ref[t], preferred_element_type=F32)
        pe_sum = h[2 * seg:2 * seg + 1, 0:hid] + h[2 * seg + 1:2 * seg + 2, hid:2 * hid]
        out_ref = kc_ref if t == 0 else vc_ref
        for kv in range(A_KV):
            first = h[kv * seg:(kv + 1) * seg, 0:hid]
            second = h[kv * seg:(kv + 1) * seg, hid:2 * hid]
            pre = first + pltpu.roll(second, seg - 1, 0) + pe_sum
            o = jnp.dot(_gelu_tanh(pre).astype(BF16), w2_ref[t], preferred_element_type=F32)
            out_ref[:, kv * A_HD:(kv + 1) * A_HD] = o[0:r_real, :]


def _cmp_weights(w_cmp1, pe_cmp, w_cmp2):
    k1 = CMP_STRIDE * A_HD
    w1 = jnp.concatenate([w_cmp1[:, 0].reshape(2, k1, A_HD), w_cmp1[:, 1].reshape(2, k1, A_HD)], axis=2).astype(BF16)
    pe = jnp.pad(pe_cmp.reshape(2, 2, k1), [(0, 0), (0, 14), (0, 0)])
    return w1, pe, w_cmp2.astype(BF16)


def _compress_prompt(kvc, b, t, w1, pe, w2):
    nr = t // CMP_STRIDE
    width = CMP_STRIDE * 2 * A_KV * A_HD
    k1 = CMP_STRIDE * A_HD
    x = kvc.reshape(b * nr, width)
    full = lambda shp: BS(shp, lambda i: (0,) * len(shp))
    return pl.pallas_call(
        functools.partial(_cmp_kernel, n_pages=1, page_rows=nr, has_new=False, n_prefetch=0, tall=False),
        out_shape=(SDS((b, nr, A_KV * A_HD), F32),) * 2,
        grid=(b,),
        in_specs=[BS((nr, width), lambda i: (i, 0)), full(pe.shape), full(w1.shape), full(w2.shape)],
        out_specs=(BS((None, nr, A_KV * A_HD), lambda i: (i, 0, 0)),) * 2,
        scratch_shapes=[pltpu.VMEM((2 * nr + 16, k1), BF16)],
        compiler_params=_cp(1),
    )(x, pe, w1, w2)


def _compress_sample(cache_cmp, page_table, kvc_new, w1, pe, w2):
    n, n_pages = page_table.shape
    page = cache_cmp.shape[1]
    pr = page // CMP_STRIDE
    width = CMP_STRIDE * 2 * A_KV * A_HD
    k1 = CMP_STRIDE * A_HD
    ts = kvc_new.shape[1]
    n_slab = 2 * A_KV
    pages = cache_cmp.reshape(cache_cmp.shape[0], page * n_slab, A_HD)
    new_blk = jnp.pad(kvc_new, [(0, 0), (0, CMP_STRIDE - ts), (0, 0)]).reshape(n, 1, width)
    nr = n_pages * pr
    seg = nr + 16
    g = np.zeros((page, page), np.float32)
    for m in range(pr):
        for j in range(CMP_STRIDE):
            g[j * pr + m, m * CMP_STRIDE + j] = 1.0
    g = jnp.asarray(g, BF16)
    full = lambda shp: BS(shp, lambda i, pt: (0,) * len(shp))
    in_specs = [BS((None, page * n_slab, A_HD), (lambda i, pt, p=p: (pt[i, p], 0, 0))) for p in range(n_pages)]
    in_specs += [BS((None, 1, width), lambda i, pt: (i, 0, 0)), full(g.shape), full(pe.shape), full(w1.shape),
                 full(w2.shape)]
    return pl.pallas_call(
        functools.partial(_cmp_kernel, n_pages=n_pages, page_rows=pr, has_new=True, n_prefetch=1, tall=True),
        out_shape=(SDS((n, nr, A_KV * A_HD), F32),) * 2,
        grid_spec=pltpu.PrefetchScalarGridSpec(
            num_scalar_prefetch=1, grid=(n,), in_specs=in_specs,
            out_specs=(BS((None, nr, A_KV * A_HD), lambda i, pt: (i, 0, 0)),) * 2,
            scratch_shapes=[pltpu.VMEM((2 * seg + 16, k1), BF16)]),
        compiler_params=_cp(1),
    )(page_table, *([pages] * n_pages), new_blk, g, pe, w1, w2)


def _cmpsel_p_kernel(q_ref, kc_ref, vc_ref, a_ref, oc_ref, sel_ref, *, tq, nc, ns, slopes):
    t0 = pl.program_id(1) * tq
    rows = A_G * tq
    qpos = t0 + lax.broadcasted_iota(jnp.int32, (rows, nc), 0) % tq
    end = lax.broadcasted_iota(jnp.int32, (rows, nc), 1) * CMP_STRIDE + (2 * CMP_STRIDE - 1)
    dist = qpos - end
    valid = dist >= 0
    distf = dist.astype(F32)
    scores = []
    for kv in range(A_KV):
        q = _stack_heads(q_ref, kv, A_G, A_HD, A_HD ** -0.5)
        kc = kc_ref[:, kv * A_HD:(kv + 1) * A_HD].astype(BF16)
        vc = vc_ref[:, kv * A_HD:(kv + 1) * A_HD].astype(BF16)
        s = _nt_dot(q, kc) - _col(slopes[kv], tq) * distf
        p, den = _masked_softmax(s, valid)
        pn = p / den
        _unstack_heads(oc_ref, jnp.dot(pn.astype(BF16), vc, preferred_element_type=F32), kv, A_G, A_HD, tq)
        psum = pn[0:tq] + pn[tq:2 * tq] + pn[2 * tq:3 * tq] + pn[3 * tq:4 * tq]
        scores.append(_select_scores(_split3_dot(psum, a_ref[...]), t0))
    sel = _topk_mask(jnp.concatenate(scores, axis=0), min(SEL_TOPK, ns))
    for kv in range(A_KV):
        sel_ref[:, kv * ns:(kv + 1) * ns] = sel[kv * tq:(kv + 1) * tq]


def _cmpsel_prompt(z, k_c, v_c, b, t, slopes):
    tq = 128
    nc = t // CMP_STRIDE
    ns = t // SEL_BLOCK
    tpb = t // tq
    a = _sel_matrix(nc, ns)
    return pl.pallas_call(
        functools.partial(_cmpsel_p_kernel, tq=tq, nc=nc, ns=ns, slopes=slopes),
        out_shape=(SDS((b * t, A_Q), F32), SDS((b * t, A_KV * ns), F32)),
        grid=(b, tpb),
        in_specs=[BS((tq, A_Q), lambda bi, qi: (bi * tpb + qi, 0)),
                  BS((None, nc, A_KV * A_HD), lambda bi, qi: (bi, 0, 0)),
                  BS((None, nc, A_KV * A_HD), lambda bi, qi: (bi, 0, 0)),
                  BS((nc, ns), lambda bi, qi: (0, 0))],
        out_specs=(BS((tq, A_Q), lambda bi, qi: (bi * tpb + qi, 0)),
                   BS((tq, A_KV * ns), lambda bi, qi: (bi * tpb + qi, 0))),
        compiler_params=_cp(2),
    )(z, k_c, v_c, a)


def _sel_p_kernel(q_ref, k_ref, v_ref, sel_ref, e_ref, o_ref, kb_ref, vb_ref, m_ref, l_ref, acc_ref, *,
                  tq, tk, ns, slopes):
    @pl.when(pl.program_id(1) == 0)
    def _():
        kb_ref[...] = k_ref[...].astype(BF16)
        vb_ref[...] = v_ref[...].astype(BF16)

    t0 = pl.program_id(1) * tq
    n_tiles = (t0 + tq - 1) // tk + 1
    qpos = t0 + lax.broadcasted_iota(jnp.int32, (tq, tk), 0)
    cc = lax.broadcasted_iota(jnp.int32, (tq, tk), 1)
    scale = A_HD ** -0.5
    for kv in range(A_KV):
        heads = [kv * A_G + g for g in range(A_G)]
        qs = [(q_ref[:, h * A_HD:(h + 1) * A_HD] * scale).astype(BF16) for h in heads]
        selk = sel_ref[:, kv * ns:(kv + 1) * ns].astype(BF16)
        m_ref[...] = jnp.full(m_ref.shape, NEG, F32)
        l_ref[...] = jnp.zeros(l_ref.shape, F32)
        acc_ref[...] = jnp.zeros(acc_ref.shape, F32)

        def body(c, carry):
            ks = pl.multiple_of(c * tk, tk)
            k = kb_ref[pl.ds(ks, tk), kv * A_HD:(kv + 1) * A_HD]
            v = vb_ref[pl.ds(ks, tk), kv * A_HD:(kv + 1) * A_HD]
            dist = qpos - (ks + cc)
            distf = dist.astype(F32)
            valid = (jnp.dot(selk, e_ref[c], preferred_element_type=F32) > 0.5) & (dist >= 0)
            for g in range(A_G):
                rs = slice(g * tq, (g + 1) * tq)
                s = jnp.where(valid, _nt_dot(qs[g], k) - slopes[kv][g] * distf, NEG)
                m_old = m_ref[rs]
                m_new = jnp.maximum(m_old, jnp.max(s, axis=1, keepdims=True))
                alpha = jnp.exp(m_old - m_new)
                p = jnp.where(valid, jnp.exp(s - m_new), 0.0)
                l_ref[rs] = alpha * l_ref[rs] + jnp.sum(p, axis=1, keepdims=True)
                acc_ref[rs] = alpha * acc_ref[rs] + jnp.dot(p.astype(BF16), v, preferred_element_type=F32)
                m_ref[rs] = m_new
            return carry

        lax.fori_loop(0, n_tiles, body, 0)
        o = acc_ref[...] / jnp.maximum(l_ref[...], TINY)
        _unstack_heads(o_ref, o, kv, A_G, A_HD, tq)


def _sel_prompt(z, sel, b, t, slopes):
    tq = 128
    tk = min(512, t)
    ns = t // SEL_BLOCK
    tpb = t // tq
    wkv = A_KV * A_HD
    e = np.zeros((t // tk, ns, tk), np.float32)
    for c in range(t // tk):
        for x in range(tk):
            e[c, (c * tk + x) // SEL_BLOCK, x] = 1.0
    rows = A_G * tq
    return pl.pallas_call(
        functools.partial(_sel_p_kernel, tq=tq, tk=tk, ns=ns, slopes=slopes),
        out_shape=SDS((b * t, A_Q), F32),
        grid=(b, tpb),
        in_specs=[BS((tq, A_Q), lambda bi, qi: (bi * tpb + qi, 0)),
                  BS((t, wkv), lambda bi, qi: (bi, L0_KVS // wkv)),
                  BS((t, wkv), lambda bi, qi: (bi, L0_KVS // wkv + 1)),
                  BS((tq, A_KV * ns), lambda bi, qi: (bi * tpb + qi, 0)),
                  BS((t // tk, ns, tk), lambda bi, qi: (0, 0, 0))],
        out_specs=BS((tq, A_Q), lambda bi, qi: (bi * tpb + qi, 0)),
        scratch_shapes=[pltpu.VMEM((t, wkv), BF16), pltpu.VMEM((t, wkv), BF16),
                        pltpu.VMEM((rows, 1), F32), pltpu.VMEM((rows, 1), F32), pltpu.VMEM((rows, A_HD), F32)],
        compiler_params=_cp(2),
    )(z, z, z, sel, jnp.asarray(e, BF16))


def _conv_p_kernel(u_ref, b_ref, c_ref, w_ref, y_ref, tail_ref, carry_ref):
    tr = u_ref.shape[0]

    @pl.when(pl.program_id(1) == 0)
    def _():
        carry_ref[...] = jnp.zeros_like(carry_ref)

    v = c_ref[...] * u_ref[...]
    ext = jnp.concatenate([carry_ref[...], v], axis=0)
    v1 = pltpu.roll(ext, 1, 0)[8:]
    v2 = pltpu.roll(ext, 2, 0)[8:]
    y_ref[...] = b_ref[...] * (w_ref[0:1, :] * v2 + w_ref[1:2, :] * v1 + w_ref[2:3, :] * v)
    carry_ref[...] = v[tr - 8:]
    tail_ref[...] = v[tr - 8:]


def _conv_prompt(z, b, t, w_conv):
    tr = min(512, t)
    tpb = t // tr
    blk = lambda off: BS((tr, B_CH), lambda bi, j: (bi * tpb + j, off // B_CH))
    return pl.pallas_call(
        _conv_p_kernel,
        out_shape=(SDS((b * t, B_CH), F32), SDS((b, 8, B_CH), F32)),
        grid=(b, tpb),
        in_specs=[blk(L0_U), blk(L0_B), blk(L0_C), BS((CONV_W, B_CH), lambda bi, j: (0, 0))],
        out_specs=(BS((tr, B_CH), lambda bi, j: (bi * tpb + j, 0)), BS((None, 8, B_CH), lambda bi, j: (bi, 0, 0))),
        scratch_shapes=[pltpu.VMEM((8, B_CH), F32)],
        compiler_params=_cp(2),
    )(z, z, z, w_conv)


def _pool_p_kernel(u_ref, wp_ref, ps_ref, y_ref, tail_ref, carry_ref):
    tr = u_ref.shape[0]
    hist = carry_ref.shape[0]

    @pl.when(pl.program_id(1) == 0)
    def _():
        carry_ref[...] = jnp.zeros_like(carry_ref)

    u = u_ref[...]
    ext = jnp.concatenate([carry_ref[...], u], axis=0)
    pos = pl.program_id(1) * tr + lax.broadcasted_iota(jnp.int32, (tr, 1), 0)
    for g, w in enumerate(POOL_WINDOWS):
        cs = slice(g * C_GRP, (g + 1) * C_GRP)
        s = ext[:, cs]
        sh = 1
        while sh < w:
            s = s + pltpu.roll(s, sh, 0)
            sh *= 2
        cnt = jnp.minimum(w, pos + 1).astype(F32)
        dlt = s[hist:] / cnt - u[:, cs]
        y_ref[:, cs] = jnp.dot(dlt.astype(BF16), wp_ref[g], preferred_element_type=F32) * ps_ref[:, cs]
    carry_ref[...] = u[tr - hist:]
    tail_ref[...] = u[tr - hist:]


def _pool_prompt(z, b, t, w_pool, pool_scale):
    tr = min(512, t)
    tpb = t // tr
    hist = 16
    return pl.pallas_call(
        _pool_p_kernel,
        out_shape=(SDS((b * t, C_CH), F32), SDS((b, hist, C_CH), F32)),
        grid=(b, tpb),
        in_specs=[BS((tr, C_CH), lambda bi, j: (bi * tpb + j, L1_U // C_CH)),
                  BS(w_pool.shape, lambda bi, j: (0, 0, 0)),
                  BS((1, C_CH), lambda bi, j: (0, 0))],
        out_specs=(BS((tr, C_CH), lambda bi, j: (bi * tpb + j, 0)), BS((None, hist, C_CH), lambda bi, j: (bi, 0, 0))),
        scratch_shapes=[pltpu.VMEM((hist, C_CH), F32)],
        compiler_params=_cp(2),
    )(z, w_pool, pool_scale.reshape(1, C_CH))


def _conv_s_kernel(u_ref, b_ref, c_ref, st_ref, w_ref, y_ref, ns_ref):
    ts = u_ref.shape[0]
    ext = [st_ref[i] for i in range(CONV_W - 1)] + [c_ref[t] * u_ref[t] for t in range(ts)]
    for t in range(ts):
        y_ref[t] = b_ref[t] * (w_ref[0:1, :] * ext[t] + w_ref[1:2, :] * ext[t + 1] + w_ref[2:3, :] * ext[t + 2])
    for i in range(CONV_W - 1):
        ns_ref[i] = ext[ts + i]


def _conv_sample(z3, state_t, w_conv):
    ts, n, _ = z3.shape
    cb = 256
    blk = lambda off: BS((ts, n, cb), lambda j: (0, 0, off // cb + j))
    return pl.pallas_call(
        _conv_s_kernel,
        out_shape=(SDS((ts, n, B_CH), F32), SDS((CONV_W - 1, n, B_CH), F32)),
        grid=(B_CH // cb,),
        in_specs=[blk(L0_U), blk(L0_B), blk(L0_C), BS((CONV_W - 1, n, cb), lambda j: (0, 0, j)),
                  BS((CONV_W, cb), lambda j: (0, j))],
        out_specs=(BS((ts, n, cb), lambda j: (0, 0, j)), BS((CONV_W - 1, n, cb), lambda j: (0, 0, j))),
        compiler_params=_cp(1),
    )(z3, z3, z3, state_t, w_conv)


def _pool_s_kernel(u_ref, st_ref, wp_ref, ps_ref, y_ref, *, past):
    ts, n, _ = u_ref.shape
    hist = st_ref.shape[0]
    for g, w in enumerate(POOL_WINDOWS):
        @pl.when(pl.program_id(0) == g)
        def _(w=w):
            ext = [st_ref[e] for e in range(hist)] + [u_ref[t] for t in range(ts)]
            ds = []
            for t in range(ts):
                acc = ext[hist + t]
                for i in range(1, w):
                    acc = acc + ext[hist + t - i]
                ds.append(acc / float(min(w, past + t + 1)) - ext[hist + t])
            y = jnp.dot(jnp.concatenate(ds, axis=0).astype(BF16), wp_ref[...], preferred_element_type=F32) * ps_ref[...]
            for t in range(ts):
                y_ref[t] = y[t * n:(t + 1) * n]


def _pool_sample(z3, state_t, w_pool, pool_scale, past):
    ts, n, _ = z3.shape
    hist = state_t.shape[0]
    ng = len(POOL_WINDOWS)
    return pl.pallas_call(
        functools.partial(_pool_s_kernel, past=past),
        out_shape=SDS((ts, n, C_CH), F32),
        grid=(ng,),
        in_specs=[BS((ts, n, C_GRP), lambda g: (0, 0, L1_U // C_GRP + g)),
                  BS((hist, n, C_GRP), lambda g: (0, 0, g)),
                  BS((None, C_GRP, C_GRP), lambda g: (g, 0, 0)),
                  BS((1, C_GRP), lambda g: (0, g))],
        out_specs=BS((ts, n, C_GRP), lambda g: (0, 0, g)),
        compiler_params=_cp(1),
    )(z3, state_t, w_pool, pool_scale.reshape(1, C_CH))


def _win_s_kernel(q_ref, st_ref, kn_ref, *rest, n_kv, g_heads, hd, win, ts, slopes, has_sink, tall):
    if has_sink:
        sink_ref, o_ref = rest
    else:
        (o_ref,) = rest
    rows = g_heads * ts
    nk = win + LANES
    ti = lax.broadcasted_iota(jnp.int32, (rows, nk), 0) % ts
    dist = ti + win - lax.broadcasted_iota(jnp.int32, (rows, nk), 1)
    valid = (dist >= 0) & (dist < win)
    distf = dist.astype(F32)
    zpad = jnp.zeros((LANES - ts, hd), F32)
    for kv in range(n_kv):
        q = _stack_heads(q_ref, kv, g_heads, hd, hd ** -0.5)
        kc = slice(kv * hd, (kv + 1) * hd)
        vc = slice((n_kv + kv) * hd, (n_kv + kv + 1) * hd)
        if tall:
            st_k = st_ref[pl.ds(kv, win, stride=2 * n_kv), :]
            st_v = st_ref[pl.ds(n_kv + kv, win, stride=2 * n_kv), :]
        else:
            st_k, st_v = st_ref[:, kc], st_ref[:, vc]
        k = jnp.concatenate([st_k, kn_ref[:, kc], zpad], axis=0).astype(BF16)
        v = jnp.concatenate([st_v, kn_ref[:, vc], zpad], axis=0).astype(BF16)
        s = _nt_dot(q, k) - _col(slopes[kv], ts) * distf
        p, den = _masked_softmax(s, valid, sink_ref[kv] if has_sink else None)
        o = jnp.dot(p.astype(BF16), v, preferred_element_type=F32) / den
        _unstack_heads(o_ref, o, kv, g_heads, hd, ts)


def _win_sample(q_nm, state, kv_new_nm, n, ts, n_kv, g_heads, hd, slopes, sinks=None):
    win = state.shape[1]
    nq = n_kv * g_heads * hd
    wkv = 2 * n_kv * hd
    tall = hd == LANES
    state = state.reshape(n, win * 2 * n_kv, hd) if tall else state.reshape(n, win, wkv)
    in_specs = [BS((ts, nq), lambda i: (i, 0)), BS((None,) + state.shape[1:], lambda i: (i, 0, 0)),
                BS((ts, wkv), lambda i: (i, 0))]
    args = [q_nm, state, kv_new_nm]
    if sinks is not None:
        sink_rows = jnp.repeat(sinks.astype(F32).reshape(n_kv, g_heads), ts, axis=1).reshape(n_kv, g_heads * ts, 1)
        in_specs.append(BS((n_kv, g_heads * ts, 1), lambda i: (0, 0, 0)))
        args.append(sink_rows)
    return pl.pallas_call(
        functools.partial(_win_s_kernel, n_kv=n_kv, g_heads=g_heads, hd=hd, win=win, ts=ts, slopes=slopes,
                          has_sink=sinks is not None, tall=tall),
        out_shape=SDS((n * ts, nq), F32),
        grid=(n,),
        in_specs=in_specs,
        out_specs=BS((ts, nq), lambda i: (i, 0)),
        compiler_params=_cp(1),
    )(*args)


def _cmpsel_s_kernel(q_ref, kc_ref, vc_ref, a_ref, oc_ref, sel_ref, *, nb, ts, nc, n_past, past, slopes):
    rows = A_G * ts
    qpos = past + lax.broadcasted_iota(jnp.int32, (rows, nc), 0) % ts
    end = lax.broadcasted_iota(jnp.int32, (rows, nc), 1) * CMP_STRIDE + (2 * CMP_STRIDE - 1)
    dist = qpos - end
    valid = dist >= 0
    distf = dist.astype(F32)
    ns_pad = a_ref.shape[1]
    lane_ok = lax.broadcasted_iota(jnp.int32, (ts, ns_pad), 1) <= n_past
    scores = []
    for i in range(nb):
        for kv in range(A_KV):
            q = _stack_heads(q_ref, kv, A_G, A_HD, A_HD ** -0.5, i * ts, ts)
            kc = kc_ref[i, :, kv * A_HD:(kv + 1) * A_HD].astype(BF16)
            vc = vc_ref[i, :, kv * A_HD:(kv + 1) * A_HD].astype(BF16)
            s = _nt_dot(q, kc) - _col(slopes[kv], ts) * distf
            p, den = _masked_softmax(s, valid)
            pn = p / den
            o = jnp.dot(pn.astype(BF16), vc, preferred_element_type=F32)
            _unstack_heads(oc_ref, o, kv, A_G, A_HD, ts, i * ts)
            psum = pn[0:ts] + pn[ts:2 * ts] + pn[2 * ts:3 * ts] + pn[3 * ts:4 * ts]
            score = _select_scores(_split3_dot(psum, a_ref[...]), past)
            scores.append(jnp.where(lane_ok, score, -2.0))
    sel = _topk_mask(jnp.concatenate(scores, axis=0), SEL_TOPK)
    for i in range(nb):
        for kv in range(A_KV):
            r0 = (i * A_KV + kv) * ts
            sel_ref[i, :, kv * n_past:(kv + 1) * n_past] = sel[r0:r0 + ts, 0:n_past]


def _cmpsel_sample(q_nm, k_c, v_c, n, ts, past, slopes):
    nc = k_c.shape[1]
    n_past = past // SEL_BLOCK
    ns_pad = -(-(n_past + 1) // LANES) * LANES
    a = _sel_matrix(nc, ns_pad)
    nb = 8
    return pl.pallas_call(
        functools.partial(_cmpsel_s_kernel, nb=nb, ts=ts, nc=nc, n_past=n_past, past=past, slopes=slopes),
        out_shape=(SDS((n * ts, A_Q), F32), SDS((n, ts, A_KV * n_past), F32)),
        grid=(n // nb,),
        in_specs=[BS((nb * ts, A_Q), lambda i: (i, 0)),
                  BS((nb, nc, A_KV * A_HD), lambda i: (i, 0, 0)),
                  BS((nb, nc, A_KV * A_HD), lambda i: (i, 0, 0)),
                  BS((nc, ns_pad), lambda i: (0, 0))],
        out_specs=(BS((nb * ts, A_Q), lambda i: (i, 0)), BS((nb, ts, A_KV * n_past), lambda i: (i, 0, 0))),
        compiler_params=_cp(1),
    )(q_nm, k_c, v_c, a)


def _sel_s_kernel(pt_ref, q_ref, sel_ref, kn_ref, e_ref, *rest, n_pages, page, ts, n_past, past, slopes):
    pages = rest[:n_pages]
    o_ref, kv_ref = rest[n_pages:]
    n_slab = 2 * A_KV
    for p in range(n_pages):
        for s in range(n_slab):
            kv_ref[p * page:(p + 1) * page, s * A_HD:(s + 1) * A_HD] = (
                pages[p][pl.ds(s, page, stride=n_slab), :].astype(BF16))
    rows = A_G * ts
    ti_p = lax.broadcasted_iota(jnp.int32, (rows, past), 0) % ts
    dist_p = (past + ti_p - lax.broadcasted_iota(jnp.int32, (rows, past), 1)).astype(F32)
    ti_n = lax.broadcasted_iota(jnp.int32, (rows, LANES), 0) % ts
    dist_n = ti_n - lax.broadcasted_iota(jnp.int32, (rows, LANES), 1)
    valid_n = dist_n >= 0
    zpad = jnp.zeros((LANES - ts, A_HD), F32)
    for kv in range(A_KV):
        q = _stack_heads(q_ref, kv, A_G, A_HD, A_HD ** -0.5)
        slope = _col(slopes[kv], ts)
        kc = slice(kv * A_HD, (kv + 1) * A_HD)
        vc = slice((A_KV + kv) * A_HD, (A_KV + kv + 1) * A_HD)
        selk = sel_ref[:, kv * n_past:(kv + 1) * n_past].astype(BF16)
        mk = jnp.dot(jnp.concatenate([selk] * A_G, axis=0), e_ref[...], preferred_element_type=F32)
        valid_p = mk > 0.5
        s_p = jnp.where(valid_p, _nt_dot(q, kv_ref[:, kc]) - slope * dist_p, NEG)
        k_n = jnp.concatenate([kn_ref[:, kc], zpad], axis=0).astype(BF16)
        v_n = jnp.concatenate([kn_ref[:, vc], zpad], axis=0).astype(BF16)
        s_n = jnp.where(valid_n, _nt_dot(q, k_n) - slope * dist_n.astype(F32), NEG)
        m = jnp.maximum(jnp.max(s_p, axis=1, keepdims=True), jnp.max(s_n, axis=1, keepdims=True))
        p_p = jnp.where(valid_p, jnp.exp(s_p - m), 0.0)
        p_n = jnp.where(valid_n, jnp.exp(s_n - m), 0.0)
        den = jnp.sum(p_p, axis=1, keepdims=True) + jnp.sum(p_n, axis=1, keepdims=True)
        o = jnp.dot(p_p.astype(BF16), kv_ref[:, vc], preferred_element_type=F32)
        o = o + jnp.dot(p_n.astype(BF16), v_n, preferred_element_type=F32)
        _unstack_heads(o_ref, o / jnp.maximum(den, TINY), kv, A_G, A_HD, ts)


def _sel_sample(q_nm, sel, kvs_new_nm, cache_sel, page_table, n, ts, past, slopes):
    n_pages = page_table.shape[1]
    page = cache_sel.shape[1]
    wkv = 2 * A_KV * A_HD
    n_past = past // SEL_BLOCK
    n_slab = 2 * A_KV
    pages = cache_sel.reshape(cache_sel.shape[0], page * n_slab, A_HD)
    e = np.zeros((n_past, past), np.float32)
    e[np.arange(past) // SEL_BLOCK, np.arange(past)] = 1.0
    in_specs = [BS((ts, A_Q), lambda i, pt: (i, 0)),
                BS((None, ts, A_KV * n_past), lambda i, pt: (i, 0, 0)),
                BS((ts, wkv), lambda i, pt: (i, 0)),
                BS((n_past, past), lambda i, pt: (0, 0))]
    in_specs += [BS((None, page * n_slab, A_HD), (lambda i, pt, p=p: (pt[i, p], 0, 0))) for p in range(n_pages)]
    return pl.pallas_call(
        functools.partial(_sel_s_kernel, n_pages=n_pages, page=page, ts=ts, n_past=n_past, past=past, slopes=slopes),
        out_shape=SDS((n * ts, A_Q), F32),
        grid_spec=pltpu.PrefetchScalarGridSpec(
            num_scalar_prefetch=1, grid=(n,), in_specs=in_specs,
            out_specs=BS((ts, A_Q), lambda i, pt: (i, 0)),
            scratch_shapes=[pltpu.VMEM((past, wkv), BF16)]),
        compiler_params=_cp(1),
    )(page_table, q_nm, sel, kvs_new_nm, jnp.asarray(e, BF16), *([pages] * n_pages))


def _to_nm(a_tm, n, ts):
    w = a_tm.shape[1]
    return a_tm.reshape(ts, n, w).transpose(1, 0, 2).reshape(n * ts, w)


def _to_tm(a_nm, n, ts):
    w = a_nm.shape[1]
    return a_nm.reshape(n, ts, w).transpose(1, 0, 2).reshape(ts * n, w)


def kernel(x_prompt, x_sample, cache_cmp_kv, cache_sel_kv, state_win_kv, state_conv, state_pool, state_swa_kv, page_table, c_prompt, c_sample, norm_g, w_ada, b_ada, w_ffn_in, w_ffn_out, final_g, w_in0, w_out0, w_cmp1, pe_cmp, w_cmp2, w_conv, w_in1, w_out1, attn_sinks, w_pool, pool_scale):
    b, t, d = x_prompt.shape
    n, ts, _ = x_sample.shape
    page = cache_cmp_kv.shape[1]
    past = page_table.shape[1] * page
    assert t % 512 == 0 or t in (128, 256), t
    assert past % SEL_BLOCK == 0 and ts <= CMP_STRIDE and past >= POOL_STATE and page % (2 * CMP_STRIDE) == 0
    assert page_table.shape[1] % 2 == 0 and ts % 8 == 0 and n % 8 == 0

    slopes_a = _slopes(A_HEADS, A_KV)
    slopes_d = _slopes(D_HEADS, D_KV)

    w0 = jnp.concatenate([w_in0[:, :A_Q], w_in0[:, A_Q + 3 * A_KVW + A_GATES:], w_in0[:, A_Q:A_Q + 3 * A_KVW],
                          w_in0[:, A_Q + 3 * A_KVW:A_Q + 3 * A_KVW + A_GATES],
                          jnp.zeros((d, LANES - A_GATES), F32)], axis=1).astype(BF16)
    w1 = jnp.concatenate([w_in1[:, :D_Q], w_in1[:, D_Q + 2 * D_KVW:], w_in1[:, D_Q:D_Q + 2 * D_KVW]], axis=1).astype(BF16)
    wo0a, wo0b = w_out0[:A_Q].astype(BF16), w_out0[A_Q:].astype(BF16)
    wo1a, wo1b = w_out1[:D_Q].astype(BF16), w_out1[D_Q:].astype(BF16)
    wf_in, wf_out = w_ffn_in.astype(BF16), w_ffn_out.astype(BF16)
    wc1, pe, wc2 = _cmp_weights(w_cmp1, pe_cmp, w_cmp2)
    wp = w_pool.astype(BF16)

    m_all = -(-(n + b) // 8) * 8
    c_all = jnp.concatenate([c_sample, c_prompt, jnp.zeros((m_all - n - b, d), F32)], axis=0)
    mods = _ada(c_all, w_ada, b_ada)

    tm_p = min(512, t)
    rows_s = n * ts
    tm_s = rows_s if rows_s <= 512 else 512
    assert tm_s % n == 0 and rows_s % tm_s == 0

    xp = x_prompt.reshape(b * t, d)
    xs = x_sample.transpose(1, 0, 2).reshape(rows_s, d)

    gp = _prompt_group(mods[0, n:n + b], b, t, d, tm_p)
    gs = _sample_group(mods[0, :n], n, ts, d, tm_s)
    xp = _ffn(xp, gp, 0, norm_g[0, 0], wf_in[0, 0], wf_out[0, 0])
    xs = _ffn(xs, gs, 0, norm_g[0, 0], wf_in[0, 0], wf_out[0, 0])
    zp = _proj(xp, gp, 1, norm_g[0, 1], w0)
    zs = _proj(xs, gs, 1, norm_g[0, 1], w0)

    cmp_p = zp[:, L0_KVC:L0_KVC + A_KVW]
    sel_p = zp[:, L0_KVS:L0_KVS + A_KVW]
    win_p = zp[:, L0_KVW:L0_KVW + A_KVW]
    kc_p, vc_p = _compress_prompt(cmp_p, b, t, wc1, pe, wc2)
    oc_p, selmask_p = _cmpsel_prompt(zp, kc_p, vc_p, b, t, slopes_a)
    os_p = _sel_prompt(zp, selmask_p, b, t, slopes_a)
    ow_p = _band_attn(zp, b, t, A_KV, A_G, A_HD, A_WIN, L0_KVW, slopes_a)
    yb_p, vtail_p = _conv_prompt(zp, b, t, w_conv)
    xp = _out0(xp, gp, oc_p, os_p, ow_p, zp, yb_p, wo0a, wo0b)

    q_nm = _to_nm(zs[:, :A_Q], n, ts)
    cmp_s = _to_nm(zs[:, L0_KVC:L0_KVC + A_KVW], n, ts)
    sel_s = _to_nm(zs[:, L0_KVS:L0_KVS + A_KVW], n, ts)
    win_s = _to_nm(zs[:, L0_KVW:L0_KVW + A_KVW], n, ts)
    kc_s, vc_s = _compress_sample(cache_cmp_kv, page_table, cmp_s.reshape(n, ts, A_KVW), wc1, pe, wc2)
    oc_s, selmask_s = _cmpsel_sample(q_nm, kc_s, vc_s, n, ts, past, slopes_a)
    os_s = _sel_sample(q_nm, selmask_s, sel_s, cache_sel_kv, page_table, n, ts, past, slopes_a)
    ow_s = _win_sample(q_nm, state_win_kv, win_s, n, ts, A_KV, A_G, A_HD, slopes_a)
    zs3 = zs.reshape(ts, n, L0_N)
    yb_s, conv_state_t = _conv_sample(zs3, state_conv.transpose(1, 0, 2), w_conv)
    xs = _out0(xs, gs, _to_tm(oc_s, n, ts), _to_tm(os_s, n, ts), _to_tm(ow_s, n, ts), zs,
               yb_s.reshape(rows_s, B_CH), wo0a, wo0b)

    xp = _ffn(xp, gp, 2, norm_g[0, 2], wf_in[0, 1], wf_out[0, 1])
    xs = _ffn(xs, gs, 2, norm_g[0, 2], wf_in[0, 1], wf_out[0, 1])

    gp = _prompt_group(mods[1, n:n + b], b, t, d, tm_p)
    gs = _sample_group(mods[1, :n], n, ts, d, tm_s)
    xp = _ffn(xp, gp, 0, norm_g[1, 0], wf_in[1, 0], wf_out[1, 0])
    xs = _ffn(xs, gs, 0, norm_g[1, 0], wf_in[1, 0], wf_out[1, 0])
    z1p = _proj(xp, gp, 1, norm_g[1, 1], w1)
    z1s = _proj(xs, gs, 1, norm_g[1, 1], w1)

    od_p = _band_attn(z1p, b, t, D_KV, D_G, D_HD, D_WIN, L1_KV, slopes_d, attn_sinks)
    yc_p, utail_p = _pool_prompt(z1p, b, t, wp, pool_scale)
    xp = _out1(xp, gp, od_p, yc_p, wo1a, wo1b)

    q1_nm = _to_nm(z1s[:, :D_Q], n, ts)
    kv1_nm = _to_nm(z1s[:, L1_KV:L1_KV + 2 * D_KVW], n, ts)
    u1_nm = _to_nm(z1s[:, L1_U:L1_U + C_CH], n, ts)
    od_s = _win_sample(q1_nm, state_swa_kv, kv1_nm, n, ts, D_KV, D_G, D_HD, slopes_d, attn_sinks)
    yc_s = _pool_sample(z1s.reshape(ts, n, L1_N), state_pool.transpose(1, 0, 2), wp, pool_scale, past)
    xs = _out1(xs, gs, _to_tm(od_s, n, ts), yc_s.reshape(rows_s, C_CH), wo1a, wo1b)

    xp = _ffn(xp, gp, 2, norm_g[1, 2], wf_in[1, 1], wf_out[1, 1])
    xs = _ffn(xs, gs, 2, norm_g[1, 2], wf_in[1, 1], wf_out[1, 1])

    y_prompt = _final_norm(xp, final_g, tm_p).reshape(b, t, d)
    y_sample = _final_norm(xs, final_g, tm_s).reshape(ts, n, d).transpose(1, 0, 2)

    kv5 = (2, A_KV, A_HD)
    cmp_kv_prompt = cmp_p.reshape(b, t, *kv5)
    sel_kv_prompt = sel_p.reshape(b, t, *kv5)
    win_kv_prompt = win_p.reshape(b, t, *kv5)[:, t - A_WIN:]
    cmp_kv_sample = cmp_s.reshape(n, ts, *kv5)
    sel_kv_sample = sel_s.reshape(n, ts, *kv5)
    win_kv_sample = jnp.concatenate([state_win_kv, win_s.reshape(n, ts, *kv5)], axis=1)[:, ts:]
    conv_prompt = vtail_p[:, 8 - (CONV_W - 1):]
    conv_sample = conv_state_t.transpose(1, 0, 2)
    pool_prompt = utail_p[:, 16 - POOL_STATE:]
    pool_sample = jnp.concatenate([state_pool, u1_nm.reshape(n, ts, C_CH)], axis=1)[:, ts:]
    swa5 = (2, D_KV, D_HD)
    swa_kv_prompt = z1p[:, L1_KV:L1_KV + 2 * D_KVW].reshape(b, t, *swa5)[:, t - D_WIN:]
    swa_kv_sample = jnp.concatenate([state_swa_kv, kv1_nm.reshape(n, ts, *swa5)], axis=1)[:, ts:]
    return (y_prompt, y_sample, cmp_kv_prompt, cmp_kv_sample, sel_kv_prompt, sel_kv_sample, win_kv_prompt,
            win_kv_sample, conv_prompt, conv_sample, pool_prompt, pool_sample, swa_kv_prompt, swa_kv_sample)
```

```python
import functools

import numpy as np
import jax
import jax.numpy as jnp
from jax import lax
from jax.experimental import pallas as pl
from jax.experimental.pallas import tpu as pltpu

F32 = jnp.float32
BF16 = jnp.bfloat16
SDS = jax.ShapeDtypeStruct
BS = pl.BlockSpec

A_HEADS, A_KV, A_HD = 8, 2, 128
A_G = A_HEADS // A_KV
CMP_STRIDE = 16
SEL_BLOCK, SEL_TOPK, SEL_LOCAL = 64, 16, 2
A_WIN = 512
B_CH, CONV_W = 1024, 3
POOL_WINDOWS = (2, 4, 8, 16)
C_CH = 1024
C_GRP = C_CH // len(POOL_WINDOWS)
POOL_STATE = max(POOL_WINDOWS) - 1
D_HEADS, D_KV, D_HD = 16, 2, 64
D_G = D_HEADS // D_KV
D_WIN = 128
N_SUB = 3
RMS_EPS = 1e-6
NEG = -1e30
FORCED = 1e9
TINY = 1e-30
A_Q = A_HEADS * A_HD
A_KVW = 2 * A_KV * A_HD
A_GATES = 3 * A_HEADS
D_Q = D_HEADS * D_HD
D_KVW = D_KV * D_HD

LANES = 128
L0_U, L0_B, L0_C = A_Q, A_Q + B_CH, A_Q + 2 * B_CH
L0_KVC = A_Q + 3 * B_CH
L0_KVS = L0_KVC + A_KVW
L0_KVW = L0_KVS + A_KVW
L0_GT = L0_KVW + A_KVW
L0_N = L0_GT + LANES
L1_U = D_Q
L1_KV = D_Q + C_CH
L1_N = L1_KV + 2 * D_KVW

VMEM_LIMIT = 56 * 2**20


def _cp(n_axes):
    return pltpu.CompilerParams(dimension_semantics=("arbitrary",) * n_axes, vmem_limit_bytes=VMEM_LIMIT)


def _pick(n, cands):
    for c in cands:
        if n % c == 0:
            return c
    return n


def _slopes(n_heads, n_kv):
    s = 2.0 ** (-8.0 * np.arange(1, n_heads + 1) / n_heads)
    return [[float(v) for v in row] for row in s.reshape(n_kv, n_heads // n_kv)]


def _rows(m, tm):
    r = m.shape[0]
    if r == 1 or r == tm:
        return m
    return jnp.concatenate([m] * (tm // r), axis=0)


def _col(vals, reps):
    return jnp.concatenate([jnp.full((reps, 1), v, F32) for v in vals], axis=0)


def _modnorm(x, g, scale, shift, tm):
    xn = x * lax.rsqrt(jnp.mean(x * x, axis=-1, keepdims=True) + RMS_EPS) * g
    return xn * (1.0 + _rows(scale, tm)) + _rows(shift, tm)


def _nt_dot(a, b):
    return lax.dot_general(a, b, (((1,), (1,)), ((), ())), preferred_element_type=F32)


def _split3_dot(p, a):
    p1 = p.astype(BF16)
    r1 = p - p1.astype(F32)
    p2 = r1.astype(BF16)
    p3 = (r1 - p2.astype(F32)).astype(BF16)
    d = lambda t: jnp.dot(t, a, preferred_element_type=F32)
    return d(p1) + d(p2) + d(p3)


def _topk_mask(score, k):
    lane = lax.broadcasted_iota(jnp.int32, score.shape, 1).astype(F32)
    big = float(score.shape[1])

    def body(_, carry):
        sc, sel = carry
        m = jnp.max(sc, axis=1, keepdims=True)
        idx = jnp.min(jnp.where(sc == m, lane, big), axis=1, keepdims=True)
        pick = lane == idx
        return jnp.where(pick, -3e38, sc), jnp.where(pick, 1.0, sel)

    _, sel = lax.fori_loop(0, k, body, (score, jnp.zeros_like(score)))
    return sel


def _select_scores(score, qpos0):
    j = lax.broadcasted_iota(jnp.int32, score.shape, 1)
    cur = (qpos0 + lax.broadcasted_iota(jnp.int32, score.shape, 0)) // SEL_BLOCK
    forced = (j == 0) | ((cur - j >= 0) & (cur - j < SEL_LOCAL))
    return jnp.where(forced, FORCED, jnp.where(j <= cur, score, -1.0))


def _sel_matrix(nc, ns_pad):
    r = SEL_BLOCK // CMP_STRIDE
    a = np.zeros((nc, ns_pad), np.float32)
    for i in range(nc):
        j = i // r
        if i % r < r - 1:
            a[i, j] = 1.0
        else:
            a[i, j] = 0.5
            if j + 1 < ns_pad:
                a[i, j + 1] = 0.5
    return jnp.asarray(a, BF16)


def _ada_kernel(c_ref, w_ref, b_ref, o_ref):
    c = c_ref[...]
    a = (c * jax.nn.sigmoid(c)).astype(BF16)
    o_ref[...] = jnp.dot(a, w_ref[...].astype(BF16), preferred_element_type=F32) + b_ref[...]


def _ada(c_all, w_ada, b_ada):
    nl, d, n9 = w_ada.shape
    m = c_all.shape[0]
    tn = _pick(n9, (1024, 512, 256, 128))
    return pl.pallas_call(
        _ada_kernel,
        out_shape=SDS((nl, m, n9), F32),
        grid=(nl, n9 // tn),
        in_specs=[BS((m, d), lambda l, j: (0, 0)),
                  BS((None, d, tn), lambda l, j: (l, 0, j)),
                  BS((None, 1, tn), lambda l, j: (l, 0, j))],
        out_specs=BS((None, m, tn), lambda l, j: (l, 0, j)),
        compiler_params=_cp(2),
    )(c_all, w_ada, b_ada.reshape(nl, 1, n9))


class _Group:
    def __init__(self, rows, tm, mods, spec_fn):
        self.rows, self.tm, self.mods, self._spec_fn = rows, tm, mods, spec_fn

    def mod_spec(self, sub, k):
        return self._spec_fn(3 * sub + k)


def _prompt_group(mods_b, b, t, d, tm):
    tpb = t // tm
    arr = mods_b.reshape(b * 3 * N_SUB, 1, d)
    return _Group(b * t, tm, arr, lambda idx: BS((None, 1, d), lambda i, *_: ((i // tpb) * (3 * N_SUB) + idx, 0, 0)))


def _sample_group(mods_n, n, ts, d, tm):
    arr = mods_n.reshape(n, 3 * N_SUB, d).transpose(1, 0, 2)
    return _Group(n * ts, tm, arr, lambda idx: BS((None, n, d), lambda i, *_: (idx, 0, 0)))


def _ffn_kernel(x_ref, sh_ref, sc_ref, gt_ref, g_ref, wg_ref, wu_ref, wo_ref, o_ref, h_ref, *, nf, n_split):
    f = pl.program_id(1)
    tm = x_ref.shape[0]

    @pl.when(f == 0)
    def _():
        h_ref[...] = _modnorm(x_ref[...], g_ref[...], sc_ref[...], sh_ref[...], tm).astype(BF16)
        o_ref[...] = jnp.zeros_like(o_ref)

    h = h_ref[...]
    tc = wg_ref.shape[1] // n_split
    chunks = [slice(c * tc, (c + 1) * tc) for c in range(n_split)]
    gu = [(jnp.dot(h, wg_ref[:, cs], preferred_element_type=F32), jnp.dot(h, wu_ref[:, cs], preferred_element_type=F32))
          for cs in chunks]
    acts = [(a * jax.nn.sigmoid(a) * u).astype(BF16) for a, u in gu]
    y = jnp.dot(acts[0], wo_ref[chunks[0], :], preferred_element_type=F32)
    for c in range(1, n_split):
        y = y + jnp.dot(acts[c], wo_ref[chunks[c], :], preferred_element_type=F32)
    o_ref[...] += y

    @pl.when(f == nf - 1)
    def _():
        o_ref[...] = x_ref[...] + 0.5 * _rows(gt_ref[...], tm) * o_ref[...]


def _ffn(x, grp, sub, g, w_in, w_out, layer, idx):
    d = x.shape[1]
    dff = w_out.shape[2]
    tf = _pick(dff, (512, 256, 128))
    nf = dff // tf
    tm = grp.tm
    n_split = 2 if tf % (2 * LANES) == 0 else 1
    return pl.pallas_call(
        functools.partial(_ffn_kernel, nf=nf, n_split=n_split),
        out_shape=SDS(x.shape, F32),
        grid=(grp.rows // tm, nf),
        in_specs=[BS((tm, d), lambda i, f: (i, 0)),
                  grp.mod_spec(sub, 0), grp.mod_spec(sub, 1), grp.mod_spec(sub, 2),
                  BS((1, d), lambda i, f: (0, 0)),
                  BS((None, None, d, tf), lambda i, f: (layer, idx, 0, f)),
                  BS((None, None, d, tf), lambda i, f: (layer, idx, 0, nf + f)),
                  BS((None, None, tf, d), lambda i, f: (layer, idx, f, 0))],
        out_specs=BS((tm, d), lambda i, f: (i, 0)),
        scratch_shapes=[pltpu.VMEM((tm, d), BF16)],
        compiler_params=_cp(2),
    )(x, grp.mods, grp.mods, grp.mods, g.reshape(1, d), w_in, w_in, w_out)


def _proj_kernel(x_ref, sh_ref, sc_ref, g_ref, w_ref, o_ref, h_ref):
    tm = x_ref.shape[0]

    @pl.when(pl.program_id(1) == 0)
    def _():
        h_ref[...] = _modnorm(x_ref[...], g_ref[...], sc_ref[...], sh_ref[...], tm).astype(BF16)

    o_ref[...] = jnp.dot(h_ref[...], w_ref[...], preferred_element_type=F32)


def _proj(x, grp, sub, g, w):
    d = x.shape[1]
    ncol = w.shape[1]
    tn = _pick(ncol, (1152, 768, 640, 512, 384, 256, 128))
    tm = grp.tm
    return pl.pallas_call(
        _proj_kernel,
        out_shape=SDS((grp.rows, ncol), F32),
        grid=(grp.rows // tm, ncol // tn),
        in_specs=[BS((tm, d), lambda i, j: (i, 0)),
                  grp.mod_spec(sub, 0), grp.mod_spec(sub, 1),
                  BS((1, d), lambda i, j: (0, 0)),
                  BS((d, tn), lambda i, j: (0, j))],
        out_specs=BS((tm, tn), lambda i, j: (i, j)),
        scratch_shapes=[pltpu.VMEM((tm, d), BF16)],
        compiler_params=_cp(2),
    )(x, grp.mods, grp.mods, g.reshape(1, d), w)


def _out0_kernel(x_ref, gt_ref, oc_ref, os_ref, ow_ref, gz_ref, yb_ref, wa_ref, wb_ref, o_ref):
    tm = x_ref.shape[0]
    gates = jax.nn.sigmoid(gz_ref[...])
    oc, osel, ow = oc_ref[...], os_ref[...], ow_ref[...]
    parts = []
    for h in range(A_HEADS):
        sl = slice(h * A_HD, (h + 1) * A_HD)
        parts.append(gates[:, 3 * h:3 * h + 1] * oc[:, sl] + gates[:, 3 * h + 1:3 * h + 2] * osel[:, sl]
                     + gates[:, 3 * h + 2:3 * h + 3] * ow[:, sl])
    oa = jnp.concatenate(parts, axis=1).astype(BF16)
    y = jnp.dot(oa, wa_ref[...], preferred_element_type=F32)
    y = y + jnp.dot(yb_ref[...].astype(BF16), wb_ref[...], preferred_element_type=F32)
    o_ref[...] = x_ref[...] + _rows(gt_ref[...], tm) * y


def _out0(x, grp, o_c, o_s, o_w, z, y_b, w_a, w_b):
    d = x.shape[1]
    tm = grp.tm
    row = lambda w: BS((tm, w), lambda i: (i, 0))
    return pl.pallas_call(
        _out0_kernel,
        out_shape=SDS(x.shape, F32),
        grid=(grp.rows // tm,),
        in_specs=[row(d), grp.mod_spec(1, 2), row(A_Q), row(A_Q), row(A_Q),
                  BS((tm, LANES), lambda i: (i, L0_GT // LANES)), row(B_CH),
                  BS((A_Q, d), lambda i: (0, 0)), BS((B_CH, d), lambda i: (0, 0))],
        out_specs=row(d),
        compiler_params=_cp(1),
    )(x, grp.mods, o_c, o_s, o_w, z, y_b, w_a, w_b)


def _out1_kernel(x_ref, gt_ref, od_ref, yc_ref, wa_ref, wb_ref, o_ref):
    tm = x_ref.shape[0]
    y = jnp.dot(od_ref[...].astype(BF16), wa_ref[...], preferred_element_type=F32)
    y = y + jnp.dot(yc_ref[...].astype(BF16), wb_ref[...], preferred_element_type=F32)
    o_ref[...] = x_ref[...] + _rows(gt_ref[...], tm) * y


def _out1(x, grp, o_d, y_c, w_a, w_b):
    d = x.shape[1]
    tm = grp.tm
    row = lambda w: BS((tm, w), lambda i: (i, 0))
    return pl.pallas_call(
        _out1_kernel,
        out_shape=SDS(x.shape, F32),
        grid=(grp.rows // tm,),
        in_specs=[row(d), grp.mod_spec(1, 2), row(D_Q), row(C_CH),
                  BS((D_Q, d), lambda i: (0, 0)), BS((C_CH, d), lambda i: (0, 0))],
        out_specs=row(d),
        compiler_params=_cp(1),
    )(x, grp.mods, o_d, y_c, w_a, w_b)


def _rms_kernel(x_ref, g_ref, o_ref):
    x = x_ref[...]
    o_ref[...] = x * lax.rsqrt(jnp.mean(x * x, axis=-1, keepdims=True) + RMS_EPS) * g_ref[...]


def _final_norm(x, g, tm):
    rows, d = x.shape
    return pl.pallas_call(
        _rms_kernel, out_shape=SDS(x.shape, F32), grid=(rows // tm,),
        in_specs=[BS((tm, d), lambda i: (i, 0)), BS((1, d), lambda i: (0, 0))],
        out_specs=BS((tm, d), lambda i: (i, 0)), compiler_params=_cp(1),
    )(x, g.reshape(1, d))


def _stack_heads(q_ref, kv, g_heads, hd, scale, r0=0, nr=None):
    rs = slice(None) if nr is None else slice(r0, r0 + nr)
    qs = [q_ref[rs, (kv * g_heads + g) * hd:(kv * g_heads + g + 1) * hd] for g in range(g_heads)]
    return (jnp.concatenate(qs, axis=0) * scale).astype(BF16)


def _unstack_heads(o_ref, o, kv, g_heads, hd, tq, r0=0):
    for g in range(g_heads):
        c0 = (kv * g_heads + g) * hd
        o_ref[r0:r0 + tq, c0:c0 + hd] = o[g * tq:(g + 1) * tq, :]


def _masked_softmax(s, valid, sink=None):
    s = jnp.where(valid, s, NEG)
    m = jnp.max(s, axis=1, keepdims=True)
    if sink is not None:
        m = jnp.maximum(m, sink)
    p = jnp.where(valid, jnp.exp(s - m), 0.0)
    den = jnp.sum(p, axis=1, keepdims=True)
    if sink is not None:
        den = den + jnp.exp(sink - m)
    return p, jnp.maximum(den, TINY)


def _band_kernel(q_ref, k_ref, v_ref, *rest, n_kv, g_heads, hd, window, tq, band, t_len, slopes, has_sink):
    if has_sink:
        sink_ref, o_ref = rest
    else:
        (o_ref,) = rest
    t0 = pl.program_id(1) * tq
    k0 = pl.multiple_of(jnp.clip(t0 - window, 0, t_len - band), tq)
    rows = g_heads * tq
    ti = lax.broadcasted_iota(jnp.int32, (rows, band), 0) % tq
    cc = lax.broadcasted_iota(jnp.int32, (rows, band), 1)
    dist = (t0 + ti) - (k0 + cc)
    valid = (dist >= 0) & (dist < window)
    distf = dist.astype(F32)
    qks = []
    for kv in range(n_kv):
        q = _stack_heads(q_ref, kv, g_heads, hd, hd ** -0.5)
        k = k_ref[pl.ds(k0, band), kv * hd:(kv + 1) * hd].astype(BF16)
        qks.append(_nt_dot(q, k))
    pds = []
    for kv in range(n_kv):
        s = qks[kv] - _col(slopes[kv], tq) * distf
        p, den = _masked_softmax(s, valid, sink_ref[kv] if has_sink else None)
        pds.append((p.astype(BF16), den))
    for kv in range(n_kv):
        v = v_ref[pl.ds(k0, band), kv * hd:(kv + 1) * hd].astype(BF16)
        o = jnp.dot(pds[kv][0], v, preferred_element_type=F32) / pds[kv][1]
        _unstack_heads(o_ref, o, kv, g_heads, hd, tq)


def _band_attn(z, b, t, n_kv, g_heads, hd, window, k_off, slopes, sinks=None):
    tq = 128
    band = min(window + tq, t)
    wkv = n_kv * hd
    nq = n_kv * g_heads * hd
    tpb = t // tq
    in_specs = [BS((tq, nq), lambda bi, qi: (bi * tpb + qi, 0)),
                BS((t, wkv), lambda bi, qi: (bi, k_off // wkv)),
                BS((t, wkv), lambda bi, qi: (bi, k_off // wkv + 1))]
    args = [z, z, z]
    if sinks is not None:
        sink_rows = jnp.repeat(sinks.astype(F32).reshape(n_kv, g_heads), tq, axis=1).reshape(n_kv, g_heads * tq, 1)
        in_specs.append(BS((n_kv, g_heads * tq, 1), lambda bi, qi: (0, 0, 0)))
        args.append(sink_rows)
    return pl.pallas_call(
        functools.partial(_band_kernel, n_kv=n_kv, g_heads=g_heads, hd=hd, window=window, tq=tq, band=band,
                          t_len=t, slopes=slopes, has_sink=sinks is not None),
        out_shape=SDS((b * t, nq), F32),
        grid=(b, tpb),
        in_specs=in_specs,
        out_specs=BS((tq, nq), lambda bi, qi: (bi * tpb + qi, 0)),
        compiler_params=_cp(2),
    )(*args)


def _gelu_tanh(x):
    return 0.5 * x * (1.0 + jnp.tanh(0.7978845608028654 * (x + 0.044715 * x * x * x)))


def _cmp_kernel(*refs, n_pages, page_rows, has_new, n_prefetch, tall):
    refs = refs[n_prefetch:]
    pages = refs[:n_pages]
    pos = n_pages
    new_ref = None
    if has_new:
        new_ref = refs[pos]
        pos += 1
    if tall:
        g_ref = refs[pos]
        pos += 1
    pe_ref, w1_ref, w2_ref, kc_ref, vc_ref, lhs_ref = refs[pos:pos + 6]
    r_real = n_pages * page_rows
    seg = r_real + (16 if has_new else 0)
    hid = A_HD
    width = 2 * A_KV * A_HD
    n_slab = 2 * A_KV

    def regroup(p, s):
        xs = pages[p][pl.ds(s, CMP_STRIDE * page_rows, stride=n_slab), :].astype(BF16)
        return jnp.dot(g_ref[...], xs, preferred_element_type=F32)

    for t in range(2):
        for kv in range(A_KV):
            s = t * A_KV + kv
            base = kv * seg
            if tall:
                for p in range(0, n_pages, 2):
                    pa, pb = regroup(p, s), regroup(p + 1, s)
                    for j in range(CMP_STRIDE):
                        rs = slice(j * page_rows, (j + 1) * page_rows)
                        pair = jnp.concatenate([pa[rs], pb[rs]], axis=0)
                        lhs_ref[base + p * page_rows:base + (p + 2) * page_rows, j * A_HD:(j + 1) * A_HD] = (
                            pair.astype(BF16))
            else:
                for j in range(CMP_STRIDE):
                    c0 = j * width + s * A_HD
                    for p in range(n_pages):
                        lhs_ref[base + p * page_rows:base + (p + 1) * page_rows, j * A_HD:(j + 1) * A_HD] = (
                            pages[p][:, c0:c0 + A_HD].astype(BF16))
            if has_new:
                for j in range(CMP_STRIDE):
                    c0 = j * width + s * A_HD
                    nb = jnp.broadcast_to(new_ref[:, c0:c0 + A_HD], (16, A_HD))
                    lhs_ref[base + r_real:base + seg, j * A_HD:(j + 1) * A_HD] = nb.astype(BF16)
        lhs_ref[2 * seg:2 * seg + 16, :] = pe_ref[t].astype(BF16)
        h = jnp.dot(lhs_ref[...], w1_ref[t], preferred_element_type=F32)
        pe_sum = h[2 * seg:2 * seg + 1, 0:hid] + h[2 * seg + 1:2 * seg + 2, hid:2 * hid]
        out_ref = kc_ref if t == 0 else vc_ref
        for kv in range(A_KV):
            first = h[kv * seg:(kv + 1) * seg, 0:hid]
            second = h[kv * seg:(kv + 1) * seg, hid:2 * hid]
            pre = first + pltpu.roll(second, seg - 1, 0) + pe_sum
            o = jnp.dot(_gelu_tanh(pre).astype(BF16), w2_ref[t], preferred_element_type=F32)
            out_ref[:, kv * A_HD:(kv + 1) * A_HD] = o[0:r_real, :]


def _cmp_weights(w_cmp1, pe_cmp, w_cmp2):
    k1 = CMP_STRIDE * A_HD
    w1 = jnp.concatenate([w_cmp1[:, 0].reshape(2, k1, A_HD), w_cmp1[:, 1].reshape(2, k1, A_HD)], axis=2).astype(BF16)
    pe = jnp.pad(pe_cmp.reshape(2, 2, k1), [(0, 0), (0, 14), (0, 0)])
    return w1, pe, w_cmp2.astype(BF16)


def _compress_prompt(kvc, b, t, w1, pe, w2):
    nr = t // CMP_STRIDE
    width = CMP_STRIDE * 2 * A_KV * A_HD
    k1 = CMP_STRIDE * A_HD
    x = kvc.reshape(b * nr, width)
    full = lambda shp: BS(shp, lambda i: (0,) * len(shp))
    return pl.pallas_call(
        functools.partial(_cmp_kernel, n_pages=1, page_rows=nr, has_new=False, n_prefetch=0, tall=False),
        out_shape=(SDS((b, nr, A_KV * A_HD), F32),) * 2,
        grid=(b,),
        in_specs=[BS((nr, width), lambda i: (i, 0)), full(pe.shape), full(w1.shape), full(w2.shape)],
        out_specs=(BS((None, nr, A_KV * A_HD), lambda i: (i, 0, 0)),) * 2,
        scratch_shapes=[pltpu.VMEM((2 * nr + 16, k1), BF16)],
        compiler_params=_cp(1),
    )(x, pe, w1, w2)


def _compress_sample(cache_cmp, page_table, kvc_new, w1, pe, w2):
    n, n_pages = page_table.shape
    page = cache_cmp.shape[1]
    pr = page // CMP_STRIDE
    width = CMP_STRIDE * 2 * A_KV * A_HD
    k1 = CMP_STRIDE * A_HD
    ts = kvc_new.shape[1]
    n_slab = 2 * A_KV
    pages = cache_cmp.reshape(cache_cmp.shape[0], page * n_slab, A_HD)
    new_blk = jnp.pad(kvc_new, [(0, 0), (0, CMP_STRIDE - ts), (0, 0)]).reshape(n, 1, width)
    nr = n_pages * pr
    seg = nr + 16
    g = np.zeros((page, page), np.float32)
    for m in range(pr):
        for j in range(CMP_STRIDE):
            g[j * pr + m, m * CMP_STRIDE + j] = 1.0
    g = jnp.asarray(g, BF16)
    full = lambda shp: BS(shp, lambda i, pt: (0,) * len(shp))
    in_specs = [BS((None, page * n_slab, A_HD), (lambda i, pt, p=p: (pt[i, p], 0, 0))) for p in range(n_pages)]
    in_specs += [BS((None, 1, width), lambda i, pt: (i, 0, 0)), full(g.shape), full(pe.shape), full(w1.shape),
                 full(w2.shape)]
    return pl.pallas_call(
        functools.partial(_cmp_kernel, n_pages=n_pages, page_rows=pr, has_new=True, n_prefetch=1, tall=True),
        out_shape=(SDS((n, nr, A_KV * A_HD), F32),) * 2,
        grid_spec=pltpu.PrefetchScalarGridSpec(
            num_scalar_prefetch=1, grid=(n,), in_specs=in_specs,
            out_specs=(BS((None, nr, A_KV * A_HD), lambda i, pt: (i, 0, 0)),) * 2,
            scratch_shapes=[pltpu.VMEM((2 * seg + 16, k1), BF16)]),
        compiler_params=_cp(1),
    )(page_table, *([pages] * n_pages), new_blk, g, pe, w1, w2)


def _cmpsel_p_kernel(q_ref, kc_ref, vc_ref, a_ref, oc_ref, sel_ref, *, tq, nc, ns, slopes):
    t0 = pl.program_id(1) * tq
    rows = A_G * tq
    qpos = t0 + lax.broadcasted_iota(jnp.int32, (rows, nc), 0) % tq
    end = lax.broadcasted_iota(jnp.int32, (rows, nc), 1) * CMP_STRIDE + (2 * CMP_STRIDE - 1)
    dist = qpos - end
    valid = dist >= 0
    distf = dist.astype(F32)
    qks = []
    for kv in range(A_KV):
        q = _stack_heads(q_ref, kv, A_G, A_HD, A_HD ** -0.5)
        qks.append(_nt_dot(q, kc_ref[:, kv * A_HD:(kv + 1) * A_HD].astype(BF16)))
    pns = []
    for kv in range(A_KV):
        p, den = _masked_softmax(qks[kv] - _col(slopes[kv], tq) * distf, valid)
        pns.append(p / den)
    scores = []
    for kv in range(A_KV):
        pn = pns[kv]
        vc = vc_ref[:, kv * A_HD:(kv + 1) * A_HD].astype(BF16)
        _unstack_heads(oc_ref, jnp.dot(pn.astype(BF16), vc, preferred_element_type=F32), kv, A_G, A_HD, tq)
        psum = pn[0:tq] + pn[tq:2 * tq] + pn[2 * tq:3 * tq] + pn[3 * tq:4 * tq]
        scores.append(_select_scores(_split3_dot(psum, a_ref[...]), t0))
    sel = _topk_mask(jnp.concatenate(scores, axis=0), min(SEL_TOPK, ns))
    for kv in range(A_KV):
        sel_ref[:, kv * ns:(kv + 1) * ns] = sel[kv * tq:(kv + 1) * tq]


def _cmpsel_prompt(z, k_c, v_c, b, t, slopes):
    tq = 128
    nc = t // CMP_STRIDE
    ns = t // SEL_BLOCK
    tpb = t // tq
    a = _sel_matrix(nc, ns)
    return pl.pallas_call(
        functools.partial(_cmpsel_p_kernel, tq=tq, nc=nc, ns=ns, slopes=slopes),
        out_shape=(SDS((b * t, A_Q), F32), SDS((b * t, A_KV * ns), F32)),
        grid=(b, tpb),
        in_specs=[BS((tq, A_Q), lambda bi, qi: (bi * tpb + qi, 0)),
                  BS((None, nc, A_KV * A_HD), lambda bi, qi: (bi, 0, 0)),
                  BS((None, nc, A_KV * A_HD), lambda bi, qi: (bi, 0, 0)),
                  BS((nc, ns), lambda bi, qi: (0, 0))],
        out_specs=(BS((tq, A_Q), lambda bi, qi: (bi * tpb + qi, 0)),
                   BS((tq, A_KV * ns), lambda bi, qi: (bi * tpb + qi, 0))),
        compiler_params=_cp(2),
    )(z, k_c, v_c, a)


def _sel_p_kernel(q_ref, k_ref, v_ref, sel_ref, e_ref, o_ref, kt_ref, vb_ref, m_ref, l_ref, acc_ref, *,
                  tq, tk, ns, slopes):
    @pl.when(pl.program_id(1) == 0)
    def _():
        for kv in range(A_KV):
            for c in range(kt_ref.shape[1]):
                kt_ref[kv, c] = k_ref[c * tk:(c + 1) * tk, kv * A_HD:(kv + 1) * A_HD].T.astype(BF16)
        vb_ref[...] = v_ref[...].astype(BF16)

    t0 = pl.program_id(1) * tq
    n_tiles = (t0 + tq - 1) // tk + 1
    qpos = t0 + lax.broadcasted_iota(jnp.int32, (tq, tk), 0)
    cc = lax.broadcasted_iota(jnp.int32, (tq, tk), 1)
    scale = A_HD ** -0.5
    for kv in range(A_KV):
        heads = [kv * A_G + g for g in range(A_G)]
        qs = [(q_ref[:, h * A_HD:(h + 1) * A_HD] * scale).astype(BF16) for h in heads]
        selk = sel_ref[:, kv * ns:(kv + 1) * ns].astype(BF16)
        m_ref[...] = jnp.full(m_ref.shape, NEG, F32)
        l_ref[...] = jnp.zeros(l_ref.shape, F32)
        acc_ref[...] = jnp.zeros(acc_ref.shape, F32)

        def body(c, carry):
            ks = pl.multiple_of(c * tk, tk)
            kt = kt_ref[kv, c]
            v = vb_ref[pl.ds(ks, tk), kv * A_HD:(kv + 1) * A_HD]
            dist = qpos - (ks + cc)
            distf = dist.astype(F32)
            valid = (jnp.dot(selk, e_ref[c], preferred_element_type=F32) > 0.5) & (dist >= 0)
            qks = [jnp.dot(qs[g], kt, preferred_element_type=F32) for g in range(A_G)]
            ps, alphas = [], []
            for g in range(A_G):
                rs = slice(g * tq, (g + 1) * tq)
                s = jnp.where(valid, qks[g] - slopes[kv][g] * distf, NEG)
                m_old = m_ref[rs]
                m_new = jnp.maximum(m_old, jnp.max(s, axis=1, keepdims=True))
                alpha = jnp.exp(m_old - m_new)
                p = jnp.exp(s - m_new)
                l_ref[rs] = alpha * l_ref[rs] + jnp.sum(p, axis=1, keepdims=True)
                m_ref[rs] = m_new
                ps.append(p.astype(BF16))
                alphas.append(alpha)
            for g in range(A_G):
                rs = slice(g * tq, (g + 1) * tq)
                acc_ref[rs] = alphas[g] * acc_ref[rs] + jnp.dot(ps[g], v, preferred_element_type=F32)
            return carry

        lax.fori_loop(0, n_tiles, body, 0)
        o = acc_ref[...] / jnp.maximum(l_ref[...], TINY)
        _unstack_heads(o_ref, o, kv, A_G, A_HD, tq)


def _sel_prompt(z, sel, b, t, slopes):
    tq = 128
    tk = min(512, t)
    ns = t // SEL_BLOCK
    tpb = t // tq
    wkv = A_KV * A_HD
    e = np.zeros((t // tk, ns, tk), np.float32)
    for c in range(t // tk):
        for x in range(tk):
            e[c, (c * tk + x) // SEL_BLOCK, x] = 1.0
    rows = A_G * tq
    return pl.pallas_call(
        functools.partial(_sel_p_kernel, tq=tq, tk=tk, ns=ns, slopes=slopes),
        out_shape=SDS((b * t, A_Q), F32),
        grid=(b, tpb),
        in_specs=[BS((tq, A_Q), lambda bi, qi: (bi * tpb + qi, 0)),
                  BS((t, wkv), lambda bi, qi: (bi, L0_KVS // wkv)),
                  BS((t, wkv), lambda bi, qi: (bi, L0_KVS // wkv + 1)),
                  BS((tq, A_KV * ns), lambda bi, qi: (bi * tpb + qi, 0)),
                  BS((t // tk, ns, tk), lambda bi, qi: (0, 0, 0))],
        out_specs=BS((tq, A_Q), lambda bi, qi: (bi * tpb + qi, 0)),
        scratch_shapes=[pltpu.VMEM((A_KV, t // tk, A_HD, tk), BF16), pltpu.VMEM((t, wkv), BF16),
                        pltpu.VMEM((rows, 1), F32), pltpu.VMEM((rows, 1), F32), pltpu.VMEM((rows, A_HD), F32)],
        compiler_params=_cp(2),
    )(z, z, z, sel, jnp.asarray(e, BF16))


def _conv_p_kernel(u_ref, b_ref, c_ref, w_ref, y_ref, tail_ref, carry_ref):
    tr = u_ref.shape[0]

    @pl.when(pl.program_id(1) == 0)
    def _():
        carry_ref[...] = jnp.zeros_like(carry_ref)

    v = c_ref[...] * u_ref[...]
    ext = jnp.concatenate([carry_ref[...], v], axis=0)
    v1 = pltpu.roll(ext, 1, 0)[8:]
    v2 = pltpu.roll(ext, 2, 0)[8:]
    y_ref[...] = b_ref[...] * (w_ref[0:1, :] * v2 + w_ref[1:2, :] * v1 + w_ref[2:3, :] * v)
    carry_ref[...] = v[tr - 8:]
    tail_ref[...] = v[tr - 8:]


def _conv_prompt(z, b, t, w_conv):
    tr = min(512, t)
    tpb = t // tr
    blk = lambda off: BS((tr, B_CH), lambda bi, j: (bi * tpb + j, off // B_CH))
    return pl.pallas_call(
        _conv_p_kernel,
        out_shape=(SDS((b * t, B_CH), F32), SDS((b, 8, B_CH), F32)),
        grid=(b, tpb),
        in_specs=[blk(L0_U), blk(L0_B), blk(L0_C), BS((CONV_W, B_CH), lambda bi, j: (0, 0))],
        out_specs=(BS((tr, B_CH), lambda bi, j: (bi * tpb + j, 0)), BS((None, 8, B_CH), lambda bi, j: (bi, 0, 0))),
        scratch_shapes=[pltpu.VMEM((8, B_CH), F32)],
        compiler_params=_cp(2),
    )(z, z, z, w_conv)


def _pool_p_kernel(u_ref, wp_ref, ps_ref, y_ref, tail_ref, carry_ref):
    tr = u_ref.shape[0]
    hist = carry_ref.shape[0]

    @pl.when(pl.program_id(1) == 0)
    def _():
        carry_ref[...] = jnp.zeros_like(carry_ref)

    u = u_ref[...]
    ext = jnp.concatenate([carry_ref[...], u], axis=0)
    pos = pl.program_id(1) * tr + lax.broadcasted_iota(jnp.int32, (tr, 1), 0)
    for g, w in enumerate(POOL_WINDOWS):
        cs = slice(g * C_GRP, (g + 1) * C_GRP)
        s = ext[:, cs]
        sh = 1
        while sh < w:
            s = s + pltpu.roll(s, sh, 0)
            sh *= 2
        cnt = jnp.minimum(w, pos + 1).astype(F32)
        dlt = s[hist:] / cnt - u[:, cs]
        y_ref[:, cs] = jnp.dot(dlt.astype(BF16), wp_ref[g], preferred_element_type=F32) * ps_ref[:, cs]
    carry_ref[...] = u[tr - hist:]
    tail_ref[...] = u[tr - hist:]


def _pool_prompt(z, b, t, w_pool, pool_scale):
    tr = min(512, t)
    tpb = t // tr
    hist = 16
    return pl.pallas_call(
        _pool_p_kernel,
        out_shape=(SDS((b * t, C_CH), F32), SDS((b, hist, C_CH), F32)),
        grid=(b, tpb),
        in_specs=[BS((tr, C_CH), lambda bi, j: (bi * tpb + j, L1_U // C_CH)),
                  BS(w_pool.shape, lambda bi, j: (0, 0, 0)),
                  BS((1, C_CH), lambda bi, j: (0, 0))],
        out_specs=(BS((tr, C_CH), lambda bi, j: (bi * tpb + j, 0)), BS((None, hist, C_CH), lambda bi, j: (bi, 0, 0))),
        scratch_shapes=[pltpu.VMEM((hist, C_CH), F32)],
        compiler_params=_cp(2),
    )(z, w_pool, pool_scale.reshape(1, C_CH))


def _conv_s_kernel(u_ref, b_ref, c_ref, st_ref, w_ref, y_ref, ns_ref):
    ts = u_ref.shape[0]
    ext = [st_ref[i] for i in range(CONV_W - 1)] + [c_ref[t] * u_ref[t] for t in range(ts)]
    for t in range(ts):
        y_ref[t] = b_ref[t] * (w_ref[0:1, :] * ext[t] + w_ref[1:2, :] * ext[t + 1] + w_ref[2:3, :] * ext[t + 2])
    for i in range(CONV_W - 1):
        ns_ref[i] = ext[ts + i]


def _conv_sample(z3, state_t, w_conv):
    ts, n, _ = z3.shape
    cb = 256
    blk = lambda off: BS((ts, n, cb), lambda j: (0, 0, off // cb + j))
    return pl.pallas_call(
        _conv_s_kernel,
        out_shape=(SDS((ts, n, B_CH), F32), SDS((CONV_W - 1, n, B_CH), F32)),
        grid=(B_CH // cb,),
        in_specs=[blk(L0_U), blk(L0_B), blk(L0_C), BS((CONV_W - 1, n, cb), lambda j: (0, 0, j)),
                  BS((CONV_W, cb), lambda j: (0, j))],
        out_specs=(BS((ts, n, cb), lambda j: (0, 0, j)), BS((CONV_W - 1, n, cb), lambda j: (0, 0, j))),
        compiler_params=_cp(1),
    )(z3, z3, z3, state_t, w_conv)


def _pool_s_kernel(u_ref, st_ref, wp_ref, ps_ref, y_ref, *, past):
    ts, n, _ = u_ref.shape
    hist = st_ref.shape[0]
    for g, w in enumerate(POOL_WINDOWS):
        @pl.when(pl.program_id(0) == g)
        def _(w=w):
            ext = [st_ref[e] for e in range(hist)] + [u_ref[t] for t in range(ts)]
            ds = []
            for t in range(ts):
                acc = ext[hist + t]
                for i in range(1, w):
                    acc = acc + ext[hist + t - i]
                ds.append(acc / float(min(w, past + t + 1)) - ext[hist + t])
            y = jnp.dot(jnp.concatenate(ds, axis=0).astype(BF16), wp_ref[...], preferred_element_type=F32) * ps_ref[...]
            for t in range(ts):
                y_ref[t] = y[t * n:(t + 1) * n]


def _pool_sample(z3, state_t, w_pool, pool_scale, past):
    ts, n, _ = z3.shape
    hist = state_t.shape[0]
    ng = len(POOL_WINDOWS)
    return pl.pallas_call(
        functools.partial(_pool_s_kernel, past=past),
        out_shape=SDS((ts, n, C_CH), F32),
        grid=(ng,),
        in_specs=[BS((ts, n, C_GRP), lambda g: (0, 0, L1_U // C_GRP + g)),
                  BS((hist, n, C_GRP), lambda g: (0, 0, g)),
                  BS((None, C_GRP, C_GRP), lambda g: (g, 0, 0)),
                  BS((1, C_GRP), lambda g: (0, g))],
        out_specs=BS((ts, n, C_GRP), lambda g: (0, 0, g)),
        compiler_params=_cp(1),
    )(z3, state_t, w_pool, pool_scale.reshape(1, C_CH))


def _win_s_kernel(q_ref, st_ref, kn_ref, *rest, n_kv, g_heads, hd, win, ts, slopes, has_sink, tall):
    if has_sink:
        sink_ref, o_ref = rest
    else:
        (o_ref,) = rest
    rows = g_heads * ts
    nk = win + LANES
    ti = lax.broadcasted_iota(jnp.int32, (rows, nk), 0) % ts
    dist = ti + win - lax.broadcasted_iota(jnp.int32, (rows, nk), 1)
    valid = (dist >= 0) & (dist < win)
    distf = dist.astype(F32)
    zpad = jnp.zeros((LANES - ts, hd), F32)
    def rows_of(slab):
        cols = slice(slab * hd, (slab + 1) * hd)
        st = st_ref[pl.ds(slab, win, stride=2 * n_kv), :] if tall else st_ref[:, cols]
        return jnp.concatenate([st, kn_ref[:, cols], zpad], axis=0).astype(BF16)

    qks = [_nt_dot(_stack_heads(q_ref, kv, g_heads, hd, hd ** -0.5), rows_of(kv)) for kv in range(n_kv)]
    pds = []
    for kv in range(n_kv):
        s = qks[kv] - _col(slopes[kv], ts) * distf
        p, den = _masked_softmax(s, valid, sink_ref[kv] if has_sink else None)
        pds.append((p.astype(BF16), den))
    for kv in range(n_kv):
        o = jnp.dot(pds[kv][0], rows_of(n_kv + kv), preferred_element_type=F32) / pds[kv][1]
        _unstack_heads(o_ref, o, kv, g_heads, hd, ts)


def _win_sample(q_nm, state, kv_new_nm, n, ts, n_kv, g_heads, hd, slopes, sinks=None):
    win = state.shape[1]
    nq = n_kv * g_heads * hd
    wkv = 2 * n_kv * hd
    tall = hd == LANES
    state = state.reshape(n, win * 2 * n_kv, hd) if tall else state.reshape(n, win, wkv)
    in_specs = [BS((ts, nq), lambda i: (i, 0)), BS((None,) + state.shape[1:], lambda i: (i, 0, 0)),
                BS((ts, wkv), lambda i: (i, 0))]
    args = [q_nm, state, kv_new_nm]
    if sinks is not None:
        sink_rows = jnp.repeat(sinks.astype(F32).reshape(n_kv, g_heads), ts, axis=1).reshape(n_kv, g_heads * ts, 1)
        in_specs.append(BS((n_kv, g_heads * ts, 1), lambda i: (0, 0, 0)))
        args.append(sink_rows)
    return pl.pallas_call(
        functools.partial(_win_s_kernel, n_kv=n_kv, g_heads=g_heads, hd=hd, win=win, ts=ts, slopes=slopes,
                          has_sink=sinks is not None, tall=tall),
        out_shape=SDS((n * ts, nq), F32),
        grid=(n,),
        in_specs=in_specs,
        out_specs=BS((ts, nq), lambda i: (i, 0)),
        compiler_params=_cp(1),
    )(*args)


def _cmpsel_s_kernel(q_ref, kc_ref, vc_ref, a_ref, oc_ref, sel_ref, *, nb, ts, nc, n_past, past, slopes):
    rows = A_G * ts
    qpos = past + lax.broadcasted_iota(jnp.int32, (rows, nc), 0) % ts
    end = lax.broadcasted_iota(jnp.int32, (rows, nc), 1) * CMP_STRIDE + (2 * CMP_STRIDE - 1)
    dist = qpos - end
    valid = dist >= 0
    distf = dist.astype(F32)
    ns_pad = a_ref.shape[1]
    lane_ok = lax.broadcasted_iota(jnp.int32, (ts, ns_pad), 1) <= n_past
    scores = []
    for i in range(nb):
        for kv in range(A_KV):
            q = _stack_heads(q_ref, kv, A_G, A_HD, A_HD ** -0.5, i * ts, ts)
            kc = kc_ref[i, :, kv * A_HD:(kv + 1) * A_HD].astype(BF16)
            vc = vc_ref[i, :, kv * A_HD:(kv + 1) * A_HD].astype(BF16)
            s = _nt_dot(q, kc) - _col(slopes[kv], ts) * distf
            p, den = _masked_softmax(s, valid)
            pn = p / den
            o = jnp.dot(pn.astype(BF16), vc, preferred_element_type=F32)
            _unstack_heads(oc_ref, o, kv, A_G, A_HD, ts, i * ts)
            psum = pn[0:ts] + pn[ts:2 * ts] + pn[2 * ts:3 * ts] + pn[3 * ts:4 * ts]
            score = _select_scores(_split3_dot(psum, a_ref[...]), past)
            scores.append(jnp.where(lane_ok, score, -2.0))
    sel = _topk_mask(jnp.concatenate(scores, axis=0), SEL_TOPK)
    for i in range(nb):
        for kv in range(A_KV):
            r0 = (i * A_KV + kv) * ts
            sel_ref[i, :, kv * n_past:(kv + 1) * n_past] = sel[r0:r0 + ts, 0:n_past]


def _cmpsel_sample(q_nm, k_c, v_c, n, ts, past, slopes):
    nc = k_c.shape[1]
    n_past = past // SEL_BLOCK
    ns_pad = -(-(n_past + 1) // LANES) * LANES
    a = _sel_matrix(nc, ns_pad)
    nb = 8
    return pl.pallas_call(
        functools.partial(_cmpsel_s_kernel, nb=nb, ts=ts, nc=nc, n_past=n_past, past=past, slopes=slopes),
        out_shape=(SDS((n * ts, A_Q), F32), SDS((n, ts, A_KV * n_past), F32)),
        grid=(n // nb,),
        in_specs=[BS((nb * ts, A_Q), lambda i: (i, 0)),
                  BS((nb, nc, A_KV * A_HD), lambda i: (i, 0, 0)),
                  BS((nb, nc, A_KV * A_HD), lambda i: (i, 0, 0)),
                  BS((nc, ns_pad), lambda i: (0, 0))],
        out_specs=(BS((nb * ts, A_Q), lambda i: (i, 0)), BS((nb, ts, A_KV * n_past), lambda i: (i, 0, 0))),
        compiler_params=_cp(1),
    )(q_nm, k_c, v_c, a)


def _sel_s_kernel(pt_ref, q_ref, sel_ref, kn_ref, e_ref, *rest, n_pages, page, ts, n_past, past, slopes):
    pages = rest[:n_pages]
    o_ref, kv_ref = rest[n_pages:]
    n_slab = 2 * A_KV
    for p in range(n_pages):
        for s in range(n_slab):
            kv_ref[p * page:(p + 1) * page, s * A_HD:(s + 1) * A_HD] = (
                pages[p][pl.ds(s, page, stride=n_slab), :].astype(BF16))
    rows = A_G * ts
    ti_p = lax.broadcasted_iota(jnp.int32, (rows, past), 0) % ts
    dist_p = (past + ti_p - lax.broadcasted_iota(jnp.int32, (rows, past), 1)).astype(F32)
    ti_n = lax.broadcasted_iota(jnp.int32, (rows, LANES), 0) % ts
    dist_n = ti_n - lax.broadcasted_iota(jnp.int32, (rows, LANES), 1)
    valid_n = dist_n >= 0
    zpad = jnp.zeros((LANES - ts, A_HD), F32)
    for kv in range(A_KV):
        q = _stack_heads(q_ref, kv, A_G, A_HD, A_HD ** -0.5)
        slope = _col(slopes[kv], ts)
        kc = slice(kv * A_HD, (kv + 1) * A_HD)
        vc = slice((A_KV + kv) * A_HD, (A_KV + kv + 1) * A_HD)
        selk = sel_ref[:, kv * n_past:(kv + 1) * n_past].astype(BF16)
        mk = jnp.dot(jnp.concatenate([selk] * A_G, axis=0), e_ref[...], preferred_element_type=F32)
        valid_p = mk > 0.5
        s_p = jnp.where(valid_p, _nt_dot(q, kv_ref[:, kc]) - slope * dist_p, NEG)
        k_n = jnp.concatenate([kn_ref[:, kc], zpad], axis=0).astype(BF16)
        v_n = jnp.concatenate([kn_ref[:, vc], zpad], axis=0).astype(BF16)
        s_n = jnp.where(valid_n, _nt_dot(q, k_n) - slope * dist_n.astype(F32), NEG)
        m = jnp.maximum(jnp.max(s_p, axis=1, keepdims=True), jnp.max(s_n, axis=1, keepdims=True))
        p_p = jnp.where(valid_p, jnp.exp(s_p - m), 0.0)
        p_n = jnp.where(valid_n, jnp.exp(s_n - m), 0.0)
        den = jnp.sum(p_p, axis=1, keepdims=True) + jnp.sum(p_n, axis=1, keepdims=True)
        o = jnp.dot(p_p.astype(BF16), kv_ref[:, vc], preferred_element_type=F32)
        o = o + jnp.dot(p_n.astype(BF16), v_n, preferred_element_type=F32)
        _unstack_heads(o_ref, o / jnp.maximum(den, TINY), kv, A_G, A_HD, ts)


def _sel_sample(q_nm, sel, kvs_new_nm, cache_sel, page_table, n, ts, past, slopes):
    n_pages = page_table.shape[1]
    page = cache_sel.shape[1]
    wkv = 2 * A_KV * A_HD
    n_past = past // SEL_BLOCK
    n_slab = 2 * A_KV
    pages = cache_sel.reshape(cache_sel.shape[0], page * n_slab, A_HD)
    e = np.zeros((n_past, past), np.float32)
    e[np.arange(past) // SEL_BLOCK, np.arange(past)] = 1.0
    in_specs = [BS((ts, A_Q), lambda i, pt: (i, 0)),
                BS((None, ts, A_KV * n_past), lambda i, pt: (i, 0, 0)),
                BS((ts, wkv), lambda i, pt: (i, 0)),
                BS((n_past, past), lambda i, pt: (0, 0))]
    in_specs += [BS((None, page * n_slab, A_HD), (lambda i, pt, p=p: (pt[i, p], 0, 0))) for p in range(n_pages)]
    return pl.pallas_call(
        functools.partial(_sel_s_kernel, n_pages=n_pages, page=page, ts=ts, n_past=n_past, past=past, slopes=slopes),
        out_shape=SDS((n * ts, A_Q), F32),
        grid_spec=pltpu.PrefetchScalarGridSpec(
            num_scalar_prefetch=1, grid=(n,), in_specs=in_specs,
            out_specs=BS((ts, A_Q), lambda i, pt: (i, 0)),
            scratch_shapes=[pltpu.VMEM((past, wkv), BF16)]),
        compiler_params=_cp(1),
    )(page_table, q_nm, sel, kvs_new_nm, jnp.asarray(e, BF16), *([pages] * n_pages))


def _to_nm(a_tm, n, ts):
    w = a_tm.shape[1]
    return a_tm.reshape(ts, n, w).transpose(1, 0, 2).reshape(n * ts, w)


def _to_tm(a_nm, n, ts):
    w = a_nm.shape[1]
    return a_nm.reshape(n, ts, w).transpose(1, 0, 2).reshape(ts * n, w)


def kernel(x_prompt, x_sample, cache_cmp_kv, cache_sel_kv, state_win_kv, state_conv, state_pool, state_swa_kv, page_table, c_prompt, c_sample, norm_g, w_ada, b_ada, w_ffn_in, w_ffn_out, final_g, w_in0, w_out0, w_cmp1, pe_cmp, w_cmp2, w_conv, w_in1, w_out1, attn_sinks, w_pool, pool_scale):
    b, t, d = x_prompt.shape
    n, ts, _ = x_sample.shape
    page = cache_cmp_kv.shape[1]
    past = page_table.shape[1] * page
    assert t % 512 == 0 or t in (128, 256), t
    assert past % SEL_BLOCK == 0 and ts <= CMP_STRIDE and past >= POOL_STATE and page % (2 * CMP_STRIDE) == 0
    assert page_table.shape[1] % 2 == 0 and ts % 8 == 0 and n % 8 == 0

    slopes_a = _slopes(A_HEADS, A_KV)
    slopes_d = _slopes(D_HEADS, D_KV)

    w0 = jnp.concatenate([w_in0[:, :A_Q], w_in0[:, A_Q + 3 * A_KVW + A_GATES:], w_in0[:, A_Q:A_Q + 3 * A_KVW],
                          w_in0[:, A_Q + 3 * A_KVW:A_Q + 3 * A_KVW + A_GATES],
                          jnp.zeros((d, LANES - A_GATES), F32)], axis=1).astype(BF16)
    w1 = jnp.concatenate([w_in1[:, :D_Q], w_in1[:, D_Q + 2 * D_KVW:], w_in1[:, D_Q:D_Q + 2 * D_KVW]], axis=1).astype(BF16)
    wo0a, wo0b = w_out0[:A_Q].astype(BF16), w_out0[A_Q:].astype(BF16)
    wo1a, wo1b = w_out1[:D_Q].astype(BF16), w_out1[D_Q:].astype(BF16)
    wf_in, wf_out = w_ffn_in.astype(BF16), w_ffn_out.astype(BF16)
    wc1, pe, wc2 = _cmp_weights(w_cmp1, pe_cmp, w_cmp2)
    wp = w_pool.astype(BF16)

    m_all = -(-(n + b) // 8) * 8
    c_all = jnp.concatenate([c_sample, c_prompt, jnp.zeros((m_all - n - b, d), F32)], axis=0)
    mods = _ada(c_all, w_ada, b_ada)

    tm_p = min(512, t)
    rows_s = n * ts
    tm_s = rows_s if rows_s <= 512 else 512
    assert tm_s % n == 0 and rows_s % tm_s == 0

    xp = x_prompt.reshape(b * t, d)
    xs = x_sample.transpose(1, 0, 2).reshape(rows_s, d)

    gp = _prompt_group(mods[0, n:n + b], b, t, d, tm_p)
    gs = _sample_group(mods[0, :n], n, ts, d, tm_s)
    xp = _ffn(xp, gp, 0, norm_g[0, 0], wf_in, wf_out, 0, 0)
    xs = _ffn(xs, gs, 0, norm_g[0, 0], wf_in, wf_out, 0, 0)
    zp = _proj(xp, gp, 1, norm_g[0, 1], w0)
    zs = _proj(xs, gs, 1, norm_g[0, 1], w0)

    cmp_p = zp[:, L0_KVC:L0_KVC + A_KVW]
    sel_p = zp[:, L0_KVS:L0_KVS + A_KVW]
    win_p = zp[:, L0_KVW:L0_KVW + A_KVW]
    kc_p, vc_p = _compress_prompt(cmp_p, b, t, wc1, pe, wc2)
    oc_p, selmask_p = _cmpsel_prompt(zp, kc_p, vc_p, b, t, slopes_a)
    os_p = _sel_prompt(zp, selmask_p, b, t, slopes_a)
    ow_p = _band_attn(zp, b, t, A_KV, A_G, A_HD, A_WIN, L0_KVW, slopes_a)
    yb_p, vtail_p = _conv_prompt(zp, b, t, w_conv)
    xp = _out0(xp, gp, oc_p, os_p, ow_p, zp, yb_p, wo0a, wo0b)

    q_nm = _to_nm(zs[:, :A_Q], n, ts)
    cmp_s = _to_nm(zs[:, L0_KVC:L0_KVC + A_KVW], n, ts)
    sel_s = _to_nm(zs[:, L0_KVS:L0_KVS + A_KVW], n, ts)
    win_s = _to_nm(zs[:, L0_KVW:L0_KVW + A_KVW], n, ts)
    kc_s, vc_s = _compress_sample(cache_cmp_kv, page_table, cmp_s.reshape(n, ts, A_KVW), wc1, pe, wc2)
    oc_s, selmask_s = _cmpsel_sample(q_nm, kc_s, vc_s, n, ts, past, slopes_a)
    os_s = _sel_sample(q_nm, selmask_s, sel_s, cache_sel_kv, page_table, n, ts, past, slopes_a)
    ow_s = _win_sample(q_nm, state_win_kv, win_s, n, ts, A_KV, A_G, A_HD, slopes_a)
    zs3 = zs.reshape(ts, n, L0_N)
    yb_s, conv_state_t = _conv_sample(zs3, state_conv.transpose(1, 0, 2), w_conv)
    xs = _out0(xs, gs, _to_tm(oc_s, n, ts), _to_tm(os_s, n, ts), _to_tm(ow_s, n, ts), zs,
               yb_s.reshape(rows_s, B_CH), wo0a, wo0b)

    xp = _ffn(xp, gp, 2, norm_g[0, 2], wf_in, wf_out, 0, 1)
    xs = _ffn(xs, gs, 2, norm_g[0, 2], wf_in, wf_out, 0, 1)

    gp = _prompt_group(mods[1, n:n + b], b, t, d, tm_p)
    gs = _sample_group(mods[1, :n], n, ts, d, tm_s)
    xp = _ffn(xp, gp, 0, norm_g[1, 0], wf_in, wf_out, 1, 0)
    xs = _ffn(xs, gs, 0, norm_g[1, 0], wf_in, wf_out, 1, 0)
    z1p = _proj(xp, gp, 1, norm_g[1, 1], w1)
    z1s = _proj(xs, gs, 1, norm_g[1, 1], w1)

    od_p = _band_attn(z1p, b, t, D_KV, D_G, D_HD, D_WIN, L1_KV, slopes_d, attn_sinks)
    yc_p, utail_p = _pool_prompt(z1p, b, t, wp, pool_scale)
    xp = _out1(xp, gp, od_p, yc_p, wo1a, wo1b)

    q1_nm = _to_nm(z1s[:, :D_Q], n, ts)
    kv1_nm = _to_nm(z1s[:, L1_KV:L1_KV + 2 * D_KVW], n, ts)
    u1_nm = _to_nm(z1s[:, L1_U:L1_U + C_CH], n, ts)
    od_s = _win_sample(q1_nm, state_swa_kv, kv1_nm, n, ts, D_KV, D_G, D_HD, slopes_d, attn_sinks)
    yc_s = _pool_sample(z1s.reshape(ts, n, L1_N), state_pool.transpose(1, 0, 2), wp, pool_scale, past)
    xs = _out1(xs, gs, _to_tm(od_s, n, ts), yc_s.reshape(rows_s, C_CH), wo1a, wo1b)

    xp = _ffn(xp, gp, 2, norm_g[1, 2], wf_in, wf_out, 1, 1)
    xs = _ffn(xs, gs, 2, norm_g[1, 2], wf_in, wf_out, 1, 1)

    y_prompt = _final_norm(xp, final_g, tm_p).reshape(b, t, d)
    y_sample = _final_norm(xs, final_g, tm_s).reshape(ts, n, d).transpose(1, 0, 2)

    kv5 = (2, A_KV, A_HD)
    cmp_kv_prompt = cmp_p.reshape(b, t, *kv5)
    sel_kv_prompt = sel_p.reshape(b, t, *kv5)
    win_kv_prompt = win_p.reshape(b, t, *kv5)[:, t - A_WIN:]
    cmp_kv_sample = cmp_s.reshape(n, ts, *kv5)
    sel_kv_sample = sel_s.reshape(n, ts, *kv5)
    win_kv_sample = jnp.concatenate([state_win_kv, win_s.reshape(n, ts, *kv5)], axis=1)[:, ts:]
    conv_prompt = vtail_p[:, 8 - (CONV_W - 1):]
    conv_sample = conv_state_t.transpose(1, 0, 2)
    pool_prompt = utail_p[:, 16 - POOL_STATE:]
    pool_sample = jnp.concatenate([state_pool, u1_nm.reshape(n, ts, C_CH)], axis=1)[:, ts:]
    swa5 = (2, D_KV, D_HD)
    swa_kv_prompt = z1p[:, L1_KV:L1_KV + 2 * D_KVW].reshape(b, t, *swa5)[:, t - D_WIN:]
    swa_kv_sample = jnp.concatenate([state_swa_kv, kv1_nm.reshape(n, ts, *swa5)], axis=1)[:, ts:]
    return (y_prompt, y_sample, cmp_kv_prompt, cmp_kv_sample, sel_kv_prompt, sel_kv_sample, win_kv_prompt,
            win_kv_sample, conv_prompt, conv_sample, pool_prompt, pool_sample, swa_kv_prompt, swa_kv_sample)
```

```python
import functools

import numpy as np
import jax
import jax.numpy as jnp
from jax import lax
from jax.experimental import pallas as pl
from jax.experimental.pallas import tpu as pltpu

F32 = jnp.float32
BF16 = jnp.bfloat16
SDS = jax.ShapeDtypeStruct
BS = pl.BlockSpec

A_HEADS, A_KV, A_HD = 8, 2, 128
A_G = A_HEADS // A_KV
CMP_STRIDE = 16
SEL_BLOCK, SEL_TOPK, SEL_LOCAL = 64, 16, 2
A_WIN = 512
B_CH, CONV_W = 1024, 3
POOL_WINDOWS = (2, 4, 8, 16)
C_CH = 1024
C_GRP = C_CH // len(POOL_WINDOWS)
POOL_STATE = max(POOL_WINDOWS) - 1
D_HEADS, D_KV, D_HD = 16, 2, 64
D_G = D_HEADS // D_KV
D_WIN = 128
N_SUB = 3
RMS_EPS = 1e-6
NEG = -1e30
FORCED = 1e9
TINY = 1e-30
A_Q = A_HEADS * A_HD
A_KVW = 2 * A_KV * A_HD
A_GATES = 3 * A_HEADS
D_Q = D_HEADS * D_HD
D_KVW = D_KV * D_HD

LANES = 128
L0_U, L0_B, L0_C = A_Q, A_Q + B_CH, A_Q + 2 * B_CH
L0_KVC = A_Q + 3 * B_CH
L0_KVS = L0_KVC + A_KVW
L0_KVW = L0_KVS + A_KVW
L0_GT = L0_KVW + A_KVW
L0_N = L0_GT + LANES
L1_U = D_Q
L1_KV = D_Q + C_CH
L1_N = L1_KV + 2 * D_KVW

VMEM_LIMIT = 56 * 2**20


def _cp(n_axes):
    return pltpu.CompilerParams(dimension_semantics=("arbitrary",) * n_axes, vmem_limit_bytes=VMEM_LIMIT)


def _pick(n, cands):
    for c in cands:
        if n % c == 0:
            return c
    return n


def _slopes(n_heads, n_kv):
    s = 2.0 ** (-8.0 * np.arange(1, n_heads + 1) / n_heads)
    return [[float(v) for v in row] for row in s.reshape(n_kv, n_heads // n_kv)]


def _rows(m, tm):
    r = m.shape[0]
    if r == 1 or r == tm:
        return m
    return jnp.concatenate([m] * (tm // r), axis=0)


def _col(vals, reps):
    return jnp.concatenate([jnp.full((reps, 1), v, F32) for v in vals], axis=0)


def _modnorm(x, g, scale, shift, tm):
    xn = x * lax.rsqrt(jnp.mean(x * x, axis=-1, keepdims=True) + RMS_EPS) * g
    return xn * (1.0 + _rows(scale, tm)) + _rows(shift, tm)


def _nt_dot(a, b):
    return lax.dot_general(a, b, (((1,), (1,)), ((), ())), preferred_element_type=F32)


def _split3_dot(p, a):
    p1 = p.astype(BF16)
    r1 = p - p1.astype(F32)
    p2 = r1.astype(BF16)
    p3 = (r1 - p2.astype(F32)).astype(BF16)
    d = lambda t: jnp.dot(t, a, preferred_element_type=F32)
    return d(p1) + d(p2) + d(p3)


def _topk_mask(score, k):
    r, c = score.shape
    cp = -(-c // LANES) * LANES
    if cp != c:
        score = jnp.concatenate([score, jnp.full((r, cp - c), -3.2e38, F32)], axis=1)
    st = score.T
    row = lax.broadcasted_iota(jnp.int32, st.shape, 0).astype(F32)

    def body(_, carry):
        sc, sel = carry
        m = jnp.max(sc, axis=0, keepdims=True)
        idx = jnp.min(jnp.where(sc == m, row, float(cp)), axis=0, keepdims=True)
        pick = row == idx
        return jnp.where(pick, -3e38, sc), jnp.where(pick, 1.0, sel)

    _, sel = lax.fori_loop(0, k, body, (st, jnp.zeros_like(st)))
    return sel.T[:, 0:c]


def _select_scores(score, qpos0):
    j = lax.broadcasted_iota(jnp.int32, score.shape, 1)
    cur = (qpos0 + lax.broadcasted_iota(jnp.int32, score.shape, 0)) // SEL_BLOCK
    forced = (j == 0) | ((cur - j >= 0) & (cur - j < SEL_LOCAL))
    return jnp.where(forced, FORCED, jnp.where(j <= cur, score, -1.0))


def _sel_matrix(nc, ns_pad):
    r = SEL_BLOCK // CMP_STRIDE
    a = np.zeros((nc, ns_pad), np.float32)
    for i in range(nc):
        j = i // r
        if i % r < r - 1:
            a[i, j] = 1.0
        else:
            a[i, j] = 0.5
            if j + 1 < ns_pad:
                a[i, j + 1] = 0.5
    return jnp.asarray(a, BF16)


def _ada_kernel(c_ref, w_ref, b_ref, o_ref):
    c = c_ref[...]
    a = (c * jax.nn.sigmoid(c)).astype(BF16)
    o_ref[...] = jnp.dot(a, w_ref[...].astype(BF16), preferred_element_type=F32) + b_ref[...]


def _ada(c_all, w_ada, b_ada):
    nl, d, n9 = w_ada.shape
    m = c_all.shape[0]
    tn = _pick(n9, (1024, 512, 256, 128))
    return pl.pallas_call(
        _ada_kernel,
        out_shape=SDS((nl, m, n9), F32),
        grid=(nl, n9 // tn),
        in_specs=[BS((m, d), lambda l, j: (0, 0)),
                  BS((None, d, tn), lambda l, j: (l, 0, j)),
                  BS((None, 1, tn), lambda l, j: (l, 0, j))],
        out_specs=BS((None, m, tn), lambda l, j: (l, 0, j)),
        compiler_params=_cp(2),
    )(c_all, w_ada, b_ada.reshape(nl, 1, n9))


class _Group:
    def __init__(self, rows, tm, mods, spec_fn):
        self.rows, self.tm, self.mods, self._spec_fn = rows, tm, mods, spec_fn

    def mod_spec(self, sub, k):
        return self._spec_fn(3 * sub + k)


def _prompt_group(mods_b, b, t, d, tm):
    tpb = t // tm
    arr = mods_b.reshape(b * 3 * N_SUB, 1, d)
    return _Group(b * t, tm, arr, lambda idx: BS((None, 1, d), lambda i, *_: ((i // tpb) * (3 * N_SUB) + idx, 0, 0)))


def _sample_group(mods_n, n, ts, d, tm):
    arr = mods_n.reshape(n, 3 * N_SUB, d).transpose(1, 0, 2)
    return _Group(n * ts, tm, arr, lambda idx: BS((None, n, d), lambda i, *_: (idx, 0, 0)))


def _ffn_kernel(x_ref, sh_ref, sc_ref, gt_ref, g_ref, fg_ref, wg_ref, wu_ref, wo_ref, o_ref, h_ref, *, nf, final_norm):
    f = pl.program_id(1)
    tm = x_ref.shape[0]

    @pl.when(f == 0)
    def _():
        h_ref[...] = _modnorm(x_ref[...], g_ref[...], sc_ref[...], sh_ref[...], tm).astype(BF16)
        o_ref[...] = jnp.zeros_like(o_ref)

    h = h_ref[...]
    a = jnp.dot(h, wg_ref[...], preferred_element_type=F32)
    u = jnp.dot(h, wu_ref[...], preferred_element_type=F32)
    act = (a * jax.nn.sigmoid(a) * u).astype(BF16)
    o_ref[...] += jnp.dot(act, wo_ref[...], preferred_element_type=F32)

    @pl.when(f == nf - 1)
    def _():
        y = x_ref[...] + 0.5 * _rows(gt_ref[...], tm) * o_ref[...]
        if final_norm:
            y = y * lax.rsqrt(jnp.mean(y * y, axis=-1, keepdims=True) + RMS_EPS) * fg_ref[...]
        o_ref[...] = y


def _ffn(x, grp, sub, g, w_in, w_out, layer, idx, final_g=None):
    d = x.shape[1]
    dff = w_out.shape[2]
    tf = _pick(dff, (512, 256, 128))
    nf = dff // tf
    tm = grp.tm
    fg = g if final_g is None else final_g
    return pl.pallas_call(
        functools.partial(_ffn_kernel, nf=nf, final_norm=final_g is not None),
        out_shape=SDS(x.shape, F32),
        grid=(grp.rows // tm, nf),
        in_specs=[BS((tm, d), lambda i, f: (i, 0)),
                  grp.mod_spec(sub, 0), grp.mod_spec(sub, 1), grp.mod_spec(sub, 2),
                  BS((1, d), lambda i, f: (0, 0)),
                  BS((1, d), lambda i, f: (0, 0)),
                  BS((None, None, d, tf), lambda i, f: (layer, idx, 0, f)),
                  BS((None, None, d, tf), lambda i, f: (layer, idx, 0, nf + f)),
                  BS((None, None, tf, d), lambda i, f: (layer, idx, f, 0))],
        out_specs=BS((tm, d), lambda i, f: (i, 0)),
        scratch_shapes=[pltpu.VMEM((tm, d), BF16)],
        compiler_params=_cp(2),
    )(x, grp.mods, grp.mods, grp.mods, g.reshape(1, d), fg.reshape(1, d), w_in, w_in, w_out)


def _proj_kernel(x_ref, sh_ref, sc_ref, g_ref, w_ref, o_ref, h_ref):
    tm = x_ref.shape[0]

    @pl.when(pl.program_id(1) == 0)
    def _():
        h_ref[...] = _modnorm(x_ref[...], g_ref[...], sc_ref[...], sh_ref[...], tm).astype(BF16)

    o_ref[...] = jnp.dot(h_ref[...], w_ref[...], preferred_element_type=F32)


def _proj(x, grp, sub, g, w):
    d = x.shape[1]
    ncol = w.shape[1]
    tn = _pick(ncol, (1152, 768, 640, 512, 384, 256, 128))
    tm = grp.tm
    return pl.pallas_call(
        _proj_kernel,
        out_shape=SDS((grp.rows, ncol), F32),
        grid=(grp.rows // tm, ncol // tn),
        in_specs=[BS((tm, d), lambda i, j: (i, 0)),
                  grp.mod_spec(sub, 0), grp.mod_spec(sub, 1),
                  BS((1, d), lambda i, j: (0, 0)),
                  BS((d, tn), lambda i, j: (0, j))],
        out_specs=BS((tm, tn), lambda i, j: (i, j)),
        scratch_shapes=[pltpu.VMEM((tm, d), BF16)],
        compiler_params=_cp(2),
    )(x, grp.mods, grp.mods, g.reshape(1, d), w)


def _out0_kernel(x_ref, gt_ref, oc_ref, os_ref, ow_ref, gz_ref, yb_ref, wa_ref, wb_ref, o_ref):
    tm = x_ref.shape[0]
    gates = jax.nn.sigmoid(gz_ref[...])
    oc, osel, ow = oc_ref[...], os_ref[...], ow_ref[...]
    parts = []
    for h in range(A_HEADS):
        sl = slice(h * A_HD, (h + 1) * A_HD)
        parts.append(gates[:, 3 * h:3 * h + 1] * oc[:, sl] + gates[:, 3 * h + 1:3 * h + 2] * osel[:, sl]
                     + gates[:, 3 * h + 2:3 * h + 3] * ow[:, sl])
    oa = jnp.concatenate(parts, axis=1).astype(BF16)
    y = jnp.dot(oa, wa_ref[...], preferred_element_type=F32)
    y = y + jnp.dot(yb_ref[...].astype(BF16), wb_ref[...], preferred_element_type=F32)
    o_ref[...] = x_ref[...] + _rows(gt_ref[...], tm) * y


def _out0(x, grp, o_c, o_s, o_w, z, y_b, w_a, w_b):
    d = x.shape[1]
    tm = grp.tm
    row = lambda w: BS((tm, w), lambda i: (i, 0))
    return pl.pallas_call(
        _out0_kernel,
        out_shape=SDS(x.shape, F32),
        grid=(grp.rows // tm,),
        in_specs=[row(d), grp.mod_spec(1, 2), row(A_Q), row(A_Q), row(A_Q),
                  BS((tm, LANES), lambda i: (i, L0_GT // LANES)), row(B_CH),
                  BS((A_Q, d), lambda i: (0, 0)), BS((B_CH, d), lambda i: (0, 0))],
        out_specs=row(d),
        compiler_params=_cp(1),
    )(x, grp.mods, o_c, o_s, o_w, z, y_b, w_a, w_b)


def _out1_kernel(x_ref, gt_ref, od_ref, yc_ref, wa_ref, wb_ref, o_ref):
    tm = x_ref.shape[0]
    y = jnp.dot(od_ref[...].astype(BF16), wa_ref[...], preferred_element_type=F32)
    y = y + jnp.dot(yc_ref[...].astype(BF16), wb_ref[...], preferred_element_type=F32)
    o_ref[...] = x_ref[...] + _rows(gt_ref[...], tm) * y


def _out1(x, grp, o_d, y_c, w_a, w_b):
    d = x.shape[1]
    tm = grp.tm
    row = lambda w: BS((tm, w), lambda i: (i, 0))
    return pl.pallas_call(
        _out1_kernel,
        out_shape=SDS(x.shape, F32),
        grid=(grp.rows // tm,),
        in_specs=[row(d), grp.mod_spec(1, 2), row(D_Q), row(C_CH),
                  BS((D_Q, d), lambda i: (0, 0)), BS((C_CH, d), lambda i: (0, 0))],
        out_specs=row(d),
        compiler_params=_cp(1),
    )(x, grp.mods, o_d, y_c, w_a, w_b)


def _stack_heads(q_ref, kv, g_heads, hd, scale, r0=0, nr=None):
    rs = slice(None) if nr is None else slice(r0, r0 + nr)
    qs = [q_ref[rs, (kv * g_heads + g) * hd:(kv * g_heads + g + 1) * hd] for g in range(g_heads)]
    return (jnp.concatenate(qs, axis=0) * scale).astype(BF16)


def _unstack_heads(o_ref, o, kv, g_heads, hd, tq, r0=0):
    for g in range(g_heads):
        c0 = (kv * g_heads + g) * hd
        o_ref[r0:r0 + tq, c0:c0 + hd] = o[g * tq:(g + 1) * tq, :]


def _masked_softmax(s, valid, sink=None):
    s = jnp.where(valid, s, NEG)
    m = jnp.max(s, axis=1, keepdims=True)
    if sink is not None:
        m = jnp.maximum(m, sink)
    p = jnp.where(valid, jnp.exp(s - m), 0.0)
    den = jnp.sum(p, axis=1, keepdims=True)
    if sink is not None:
        den = den + jnp.exp(sink - m)
    return p, jnp.maximum(den, TINY)


def _band_kernel(q_ref, k_ref, v_ref, *rest, n_kv, g_heads, hd, window, tq, band, t_len, slopes, has_sink):
    if has_sink:
        sink_ref, o_ref = rest
    else:
        (o_ref,) = rest
    t0 = pl.program_id(1) * tq
    k0 = pl.multiple_of(jnp.clip(t0 - window, 0, t_len - band), tq)
    rows = g_heads * tq
    ti = lax.broadcasted_iota(jnp.int32, (rows, band), 0) % tq
    cc = lax.broadcasted_iota(jnp.int32, (rows, band), 1)
    dist = (t0 + ti) - (k0 + cc)
    valid = (dist >= 0) & (dist < window)
    distf = dist.astype(F32)
    qks = []
    for kv in range(n_kv):
        q = _stack_heads(q_ref, kv, g_heads, hd, hd ** -0.5)
        k = k_ref[pl.ds(k0, band), kv * hd:(kv + 1) * hd].astype(BF16)
        qks.append(_nt_dot(q, k))
    pds = []
    for kv in range(n_kv):
        s = qks[kv] - _col(slopes[kv], tq) * distf
        p, den = _masked_softmax(s, valid, sink_ref[kv] if has_sink else None)
        pds.append((p.astype(BF16), den))
    for kv in range(n_kv):
        v = v_ref[pl.ds(k0, band), kv * hd:(kv + 1) * hd].astype(BF16)
        o = jnp.dot(pds[kv][0], v, preferred_element_type=F32) / pds[kv][1]
        _unstack_heads(o_ref, o, kv, g_heads, hd, tq)


def _band_attn(z, b, t, n_kv, g_heads, hd, window, k_off, slopes, sinks=None):
    tq = 128
    band = min(window + tq, t)
    wkv = n_kv * hd
    nq = n_kv * g_heads * hd
    tpb = t // tq
    in_specs = [BS((tq, nq), lambda bi, qi: (bi * tpb + qi, 0)),
                BS((t, wkv), lambda bi, qi: (bi, k_off // wkv)),
                BS((t, wkv), lambda bi, qi: (bi, k_off // wkv + 1))]
    args = [z, z, z]
    if sinks is not None:
        sink_rows = jnp.repeat(sinks.astype(F32).reshape(n_kv, g_heads), tq, axis=1).reshape(n_kv, g_heads * tq, 1)
        in_specs.append(BS((n_kv, g_heads * tq, 1), lambda bi, qi: (0, 0, 0)))
        args.append(sink_rows)
    return pl.pallas_call(
        functools.partial(_band_kernel, n_kv=n_kv, g_heads=g_heads, hd=hd, window=window, tq=tq, band=band,
                          t_len=t, slopes=slopes, has_sink=sinks is not None),
        out_shape=SDS((b * t, nq), F32),
        grid=(b, tpb),
        in_specs=in_specs,
        out_specs=BS((tq, nq), lambda bi, qi: (bi * tpb + qi, 0)),
        compiler_params=_cp(2),
    )(*args)


def _gelu_tanh(x):
    return 0.5 * x * (1.0 + jnp.tanh(0.7978845608028654 * (x + 0.044715 * x * x * x)))


def _cmp_kernel(*refs, n_pages, page_rows, has_new, n_prefetch, tall):
    refs = refs[n_prefetch:]
    pages = refs[:n_pages]
    pos = n_pages
    new_ref = None
    if has_new:
        new_ref = refs[pos]
        pos += 1
    if tall:
        g_ref = refs[pos]
        pos += 1
    pe_ref, w1_ref, w2_ref, kc_ref, vc_ref, lhs_ref = refs[pos:pos + 6]
    r_real = n_pages * page_rows
    seg = r_real + (16 if has_new else 0)
    hid = A_HD
    width = 2 * A_KV * A_HD
    n_slab = 2 * A_KV

    def regroup(p, s):
        xs = pages[p][pl.ds(s, CMP_STRIDE * page_rows, stride=n_slab), :].astype(BF16)
        return jnp.dot(g_ref[...], xs, preferred_element_type=F32)

    for t in range(2):
        for kv in range(A_KV):
            s = t * A_KV + kv
            base = kv * seg
            if tall:
                for p in range(0, n_pages, 2):
                    pa, pb = regroup(p, s), regroup(p + 1, s)
                    for j in range(CMP_STRIDE):
                        rs = slice(j * page_rows, (j + 1) * page_rows)
                        pair = jnp.concatenate([pa[rs], pb[rs]], axis=0)
                        lhs_ref[base + p * page_rows:base + (p + 2) * page_rows, j * A_HD:(j + 1) * A_HD] = (
                            pair.astype(BF16))
            else:
                for j in range(CMP_STRIDE):
                    c0 = j * width + s * A_HD
                    for p in range(n_pages):
                        lhs_ref[base + p * page_rows:base + (p + 1) * page_rows, j * A_HD:(j + 1) * A_HD] = (
                            pages[p][:, c0:c0 + A_HD].astype(BF16))
            if has_new:
                for j in range(CMP_STRIDE):
                    c0 = j * width + s * A_HD
                    nb = jnp.broadcast_to(new_ref[:, c0:c0 + A_HD], (16, A_HD))
                    lhs_ref[base + r_real:base + seg, j * A_HD:(j + 1) * A_HD] = nb.astype(BF16)
        lhs_ref[2 * seg:2 * seg + 16, :] = pe_ref[t].astype(BF16)
        h = jnp.dot(lhs_ref[...], w1_ref[t], preferred_element_type=F32)
        pe_sum = h[2 * seg:2 * seg + 1, 0:hid] + h[2 * seg + 1:2 * seg + 2, hid:2 * hid]
        out_ref = kc_ref if t == 0 else vc_ref
        for kv in range(A_KV):
            first = h[kv * seg:(kv + 1) * seg, 0:hid]
            second = h[kv * seg:(kv + 1) * seg, hid:2 * hid]
            pre = first + pltpu.roll(second, seg - 1, 0) + pe_sum
            o = jnp.dot(_gelu_tanh(pre).astype(BF16), w2_ref[t], preferred_element_type=F32)
            out_ref[:, kv * A_HD:(kv + 1) * A_HD] = o[0:r_real, :]


def _cmp_weights(w_cmp1, pe_cmp, w_cmp2):
    k1 = CMP_STRIDE * A_HD
    w1 = jnp.concatenate([w_cmp1[:, 0].reshape(2, k1, A_HD), w_cmp1[:, 1].reshape(2, k1, A_HD)], axis=2).astype(BF16)
    pe = jnp.pad(pe_cmp.reshape(2, 2, k1), [(0, 0), (0, 14), (0, 0)])
    return w1, pe, w_cmp2.astype(BF16)


def _compress_prompt(kvc, b, t, w1, pe, w2):
    nr = t // CMP_STRIDE
    width = CMP_STRIDE * 2 * A_KV * A_HD
    k1 = CMP_STRIDE * A_HD
    x = kvc.reshape(b * nr, width)
    full = lambda shp: BS(shp, lambda i: (0,) * len(shp))
    return pl.pallas_call(
        functools.partial(_cmp_kernel, n_pages=1, page_rows=nr, has_new=False, n_prefetch=0, tall=False),
        out_shape=(SDS((b, nr, A_KV * A_HD), F32),) * 2,
        grid=(b,),
        in_specs=[BS((nr, width), lambda i: (i, 0)), full(pe.shape), full(w1.shape), full(w2.shape)],
        out_specs=(BS((None, nr, A_KV * A_HD), lambda i: (i, 0, 0)),) * 2,
        scratch_shapes=[pltpu.VMEM((2 * nr + 16, k1), BF16)],
        compiler_params=_cp(1),
    )(x, pe, w1, w2)


def _compress_sample(cache_cmp, page_table, kvc_new, w1, pe, w2):
    n, n_pages = page_table.shape
    page = cache_cmp.shape[1]
    pr = page // CMP_STRIDE
    width = CMP_STRIDE * 2 * A_KV * A_HD
    k1 = CMP_STRIDE * A_HD
    ts = kvc_new.shape[1]
    n_slab = 2 * A_KV
    pages = cache_cmp.reshape(cache_cmp.shape[0], page * n_slab, A_HD)
    new_blk = jnp.pad(kvc_new, [(0, 0), (0, CMP_STRIDE - ts), (0, 0)]).reshape(n, 1, width)
    nr = n_pages * pr
    seg = nr + 16
    g = np.zeros((page, page), np.float32)
    for m in range(pr):
        for j in range(CMP_STRIDE):
            g[j * pr + m, m * CMP_STRIDE + j] = 1.0
    g = jnp.asarray(g, BF16)
    full = lambda shp: BS(shp, lambda i, pt: (0,) * len(shp))
    in_specs = [BS((None, page * n_slab, A_HD), (lambda i, pt, p=p: (pt[i, p], 0, 0))) for p in range(n_pages)]
    in_specs += [BS((None, 1, width), lambda i, pt: (i, 0, 0)), full(g.shape), full(pe.shape), full(w1.shape),
                 full(w2.shape)]
    return pl.pallas_call(
        functools.partial(_cmp_kernel, n_pages=n_pages, page_rows=pr, has_new=True, n_prefetch=1, tall=True),
        out_shape=(SDS((n, nr, A_KV * A_HD), F32),) * 2,
        grid_spec=pltpu.PrefetchScalarGridSpec(
            num_scalar_prefetch=1, grid=(n,), in_specs=in_specs,
            out_specs=(BS((None, nr, A_KV * A_HD), lambda i, pt: (i, 0, 0)),) * 2,
            scratch_shapes=[pltpu.VMEM((2 * seg + 16, k1), BF16)]),
        compiler_params=_cp(1),
    )(page_table, *([pages] * n_pages), new_blk, g, pe, w1, w2)


def _cmpsel_p_kernel(q_ref, kc_ref, vc_ref, a_ref, oc_ref, sel_ref, *, tq, nc, ns, slopes):
    t0 = pl.program_id(1) * tq
    rows = A_G * tq
    qpos = t0 + lax.broadcasted_iota(jnp.int32, (rows, nc), 0) % tq
    end = lax.broadcasted_iota(jnp.int32, (rows, nc), 1) * CMP_STRIDE + (2 * CMP_STRIDE - 1)
    dist = qpos - end
    valid = dist >= 0
    distf = dist.astype(F32)
    qks = []
    for kv in range(A_KV):
        q = _stack_heads(q_ref, kv, A_G, A_HD, A_HD ** -0.5)
        qks.append(_nt_dot(q, kc_ref[:, kv * A_HD:(kv + 1) * A_HD].astype(BF16)))
    pns = []
    for kv in range(A_KV):
        p, den = _masked_softmax(qks[kv] - _col(slopes[kv], tq) * distf, valid)
        pns.append(p / den)
    scores = []
    for kv in range(A_KV):
        pn = pns[kv]
        vc = vc_ref[:, kv * A_HD:(kv + 1) * A_HD].astype(BF16)
        _unstack_heads(oc_ref, jnp.dot(pn.astype(BF16), vc, preferred_element_type=F32), kv, A_G, A_HD, tq)
        psum = pn[0:tq] + pn[tq:2 * tq] + pn[2 * tq:3 * tq] + pn[3 * tq:4 * tq]
        scores.append(_select_scores(_split3_dot(psum, a_ref[...]), t0))
    sel = _topk_mask(jnp.concatenate(scores, axis=0), min(SEL_TOPK, ns))
    for kv in range(A_KV):
        sel_ref[:, kv * ns:(kv + 1) * ns] = sel[kv * tq:(kv + 1) * tq]


def _cmpsel_prompt(z, k_c, v_c, b, t, slopes):
    tq = 128
    nc = t // CMP_STRIDE
    ns = t // SEL_BLOCK
    tpb = t // tq
    a = _sel_matrix(nc, ns)
    return pl.pallas_call(
        functools.partial(_cmpsel_p_kernel, tq=tq, nc=nc, ns=ns, slopes=slopes),
        out_shape=(SDS((b * t, A_Q), F32), SDS((b * t, A_KV * ns), F32)),
        grid=(b, tpb),
        in_specs=[BS((tq, A_Q), lambda bi, qi: (bi * tpb + qi, 0)),
                  BS((None, nc, A_KV * A_HD), lambda bi, qi: (bi, 0, 0)),
                  BS((None, nc, A_KV * A_HD), lambda bi, qi: (bi, 0, 0)),
                  BS((nc, ns), lambda bi, qi: (0, 0))],
        out_specs=(BS((tq, A_Q), lambda bi, qi: (bi * tpb + qi, 0)),
                   BS((tq, A_KV * ns), lambda bi, qi: (bi * tpb + qi, 0))),
        compiler_params=_cp(2),
    )(z, k_c, v_c, a)


def _sel_p_kernel(q_ref, k_ref, v_ref, sel_ref, e_ref, o_ref, kt_ref, vb_ref, m_ref, l_ref, acc_ref, s_ref, p_ref,
                  al_ref, *, tq, tk, ns, slopes):
    @pl.when(pl.program_id(1) == 0)
    def _():
        for kv in range(A_KV):
            for c in range(kt_ref.shape[1]):
                kt_ref[kv, c] = k_ref[c * tk:(c + 1) * tk, kv * A_HD:(kv + 1) * A_HD].T.astype(BF16)
        vb_ref[...] = v_ref[...].astype(BF16)

    t0 = pl.program_id(1) * tq
    n_tiles = (t0 + tq - 1) // tk + 1
    qpos = t0 + lax.broadcasted_iota(jnp.int32, (tq, tk), 0)
    cc = lax.broadcasted_iota(jnp.int32, (tq, tk), 1)
    scale = A_HD ** -0.5
    for kv in range(A_KV):
        heads = [kv * A_G + g for g in range(A_G)]
        qs = [(q_ref[:, h * A_HD:(h + 1) * A_HD] * scale).astype(BF16) for h in heads]
        selk = sel_ref[:, kv * ns:(kv + 1) * ns].astype(BF16)
        m_ref[...] = jnp.full(m_ref.shape, NEG, F32)
        l_ref[...] = jnp.zeros(l_ref.shape, F32)
        acc_ref[...] = jnp.zeros(acc_ref.shape, F32)
        p_ref[1] = jnp.zeros(p_ref.shape[1:], BF16)
        al_ref[1] = jnp.ones(al_ref.shape[1:], F32)

        def scores(c, slot):
            kt = kt_ref[kv, c]
            for g in range(A_G):
                s_ref[slot, g * tq:(g + 1) * tq] = jnp.dot(qs[g], kt, preferred_element_type=F32)

        def values(c, slot):
            v = vb_ref[pl.ds(pl.multiple_of(c * tk, tk), tk), kv * A_HD:(kv + 1) * A_HD]
            for g in range(A_G):
                rs = slice(g * tq, (g + 1) * tq)
                acc_ref[rs] = al_ref[slot, rs] * acc_ref[rs] + jnp.dot(p_ref[slot, rs], v, preferred_element_type=F32)

        def softmax(c, slot):
            dist = qpos - (c * tk + cc)
            distf = dist.astype(F32)
            valid = (jnp.dot(selk, e_ref[c], preferred_element_type=F32) > 0.5) & (dist >= 0)
            for g in range(A_G):
                rs = slice(g * tq, (g + 1) * tq)
                s = jnp.where(valid, s_ref[slot, rs] - slopes[kv][g] * distf, NEG)
                m_old = m_ref[rs]
                m_new = jnp.maximum(m_old, jnp.max(s, axis=1, keepdims=True))
                alpha = jnp.exp(m_old - m_new)
                p = jnp.exp(s - m_new)
                l_ref[rs] = alpha * l_ref[rs] + jnp.sum(p, axis=1, keepdims=True)
                m_ref[rs] = m_new
                p_ref[slot, rs] = p.astype(BF16)
                al_ref[slot, rs] = alpha

        scores(0, 0)

        def body(c, carry):
            for par in range(2):
                @pl.when(c % 2 == par)
                def _(par=par):
                    scores(jnp.minimum(c + 1, n_tiles - 1), 1 - par)
                    values(jnp.maximum(c - 1, 0), 1 - par)
                    softmax(c, par)
            return carry

        lax.fori_loop(0, n_tiles, body, 0)
        for par in range(2):
            @pl.when((n_tiles - 1) % 2 == par)
            def _(par=par):
                values(n_tiles - 1, par)
        o = acc_ref[...] / jnp.maximum(l_ref[...], TINY)
        _unstack_heads(o_ref, o, kv, A_G, A_HD, tq)


def _sel_prompt(z, sel, b, t, slopes):
    tq = 128
    tk = min(512, t)
    ns = t // SEL_BLOCK
    tpb = t // tq
    wkv = A_KV * A_HD
    e = np.zeros((t // tk, ns, tk), np.float32)
    for c in range(t // tk):
        for x in range(tk):
            e[c, (c * tk + x) // SEL_BLOCK, x] = 1.0
    rows = A_G * tq
    return pl.pallas_call(
        functools.partial(_sel_p_kernel, tq=tq, tk=tk, ns=ns, slopes=slopes),
        out_shape=SDS((b * t, A_Q), F32),
        grid=(b, tpb),
        in_specs=[BS((tq, A_Q), lambda bi, qi: (bi * tpb + qi, 0)),
                  BS((t, wkv), lambda bi, qi: (bi, L0_KVS // wkv)),
                  BS((t, wkv), lambda bi, qi: (bi, L0_KVS // wkv + 1)),
                  BS((tq, A_KV * ns), lambda bi, qi: (bi * tpb + qi, 0)),
                  BS((t // tk, ns, tk), lambda bi, qi: (0, 0, 0))],
        out_specs=BS((tq, A_Q), lambda bi, qi: (bi * tpb + qi, 0)),
        scratch_shapes=[pltpu.VMEM((A_KV, t // tk, A_HD, tk), BF16), pltpu.VMEM((t, wkv), BF16),
                        pltpu.VMEM((rows, 1), F32), pltpu.VMEM((rows, 1), F32), pltpu.VMEM((rows, A_HD), F32),
                        pltpu.VMEM((2, rows, tk), F32), pltpu.VMEM((2, rows, tk), BF16), pltpu.VMEM((2, rows, 1), F32)],
        compiler_params=_cp(2),
    )(z, z, z, sel, jnp.asarray(e, BF16))


def _conv_p_kernel(u_ref, b_ref, c_ref, w_ref, y_ref, tail_ref, carry_ref):
    tr = u_ref.shape[0]

    @pl.when(pl.program_id(1) == 0)
    def _():
        carry_ref[...] = jnp.zeros_like(carry_ref)

    v = c_ref[...] * u_ref[...]
    ext = jnp.concatenate([carry_ref[...], v], axis=0)
    v1 = pltpu.roll(ext, 1, 0)[8:]
    v2 = pltpu.roll(ext, 2, 0)[8:]
    y_ref[...] = b_ref[...] * (w_ref[0:1, :] * v2 + w_ref[1:2, :] * v1 + w_ref[2:3, :] * v)
    carry_ref[...] = v[tr - 8:]
    tail_ref[...] = v[tr - 8:]


def _conv_prompt(z, b, t, w_conv):
    tr = min(512, t)
    tpb = t // tr
    blk = lambda off: BS((tr, B_CH), lambda bi, j: (bi * tpb + j, off // B_CH))
    return pl.pallas_call(
        _conv_p_kernel,
        out_shape=(SDS((b * t, B_CH), F32), SDS((b, 8, B_CH), F32)),
        grid=(b, tpb),
        in_specs=[blk(L0_U), blk(L0_B), blk(L0_C), BS((CONV_W, B_CH), lambda bi, j: (0, 0))],
        out_specs=(BS((tr, B_CH), lambda bi, j: (bi * tpb + j, 0)), BS((None, 8, B_CH), lambda bi, j: (bi, 0, 0))),
        scratch_shapes=[pltpu.VMEM((8, B_CH), F32)],
        compiler_params=_cp(2),
    )(z, z, z, w_conv)


def _pool_p_kernel(u_ref, wp_ref, ps_ref, y_ref, tail_ref, carry_ref):
    tr = u_ref.shape[0]
    hist = carry_ref.shape[0]

    @pl.when(pl.program_id(1) == 0)
    def _():
        carry_ref[...] = jnp.zeros_like(carry_ref)

    u = u_ref[...]
    ext = jnp.concatenate([carry_ref[...], u], axis=0)
    pos = pl.program_id(1) * tr + lax.broadcasted_iota(jnp.int32, (tr, 1), 0)
    for g, w in enumerate(POOL_WINDOWS):
        cs = slice(g * C_GRP, (g + 1) * C_GRP)
        s = ext[:, cs]
        sh = 1
        while sh < w:
            s = s + pltpu.roll(s, sh, 0)
            sh *= 2
        cnt = jnp.minimum(w, pos + 1).astype(F32)
        dlt = s[hist:] / cnt - u[:, cs]
        y_ref[:, cs] = jnp.dot(dlt.astype(BF16), wp_ref[g], preferred_element_type=F32) * ps_ref[:, cs]
    carry_ref[...] = u[tr - hist:]
    tail_ref[...] = u[tr - hist:]


def _pool_prompt(z, b, t, w_pool, pool_scale):
    tr = min(512, t)
    tpb = t // tr
    hist = 16
    return pl.pallas_call(
        _pool_p_kernel,
        out_shape=(SDS((b * t, C_CH), F32), SDS((b, hist, C_CH), F32)),
        grid=(b, tpb),
        in_specs=[BS((tr, C_CH), lambda bi, j: (bi * tpb + j, L1_U // C_CH)),
                  BS(w_pool.shape, lambda bi, j: (0, 0, 0)),
                  BS((1, C_CH), lambda bi, j: (0, 0))],
        out_specs=(BS((tr, C_CH), lambda bi, j: (bi * tpb + j, 0)), BS((None, hist, C_CH), lambda bi, j: (bi, 0, 0))),
        scratch_shapes=[pltpu.VMEM((hist, C_CH), F32)],
        compiler_params=_cp(2),
    )(z, w_pool, pool_scale.reshape(1, C_CH))


def _conv_s_kernel(u_ref, b_ref, c_ref, st_ref, w_ref, y_ref, ns_ref):
    ts = u_ref.shape[0]
    ext = [st_ref[i] for i in range(CONV_W - 1)] + [c_ref[t] * u_ref[t] for t in range(ts)]
    for t in range(ts):
        y_ref[t] = b_ref[t] * (w_ref[0:1, :] * ext[t] + w_ref[1:2, :] * ext[t + 1] + w_ref[2:3, :] * ext[t + 2])
    for i in range(CONV_W - 1):
        ns_ref[i] = ext[ts + i]


def _conv_sample(z3, state_t, w_conv):
    ts, n, _ = z3.shape
    cb = 256
    blk = lambda off: BS((ts, n, cb), lambda j: (0, 0, off // cb + j))
    return pl.pallas_call(
        _conv_s_kernel,
        out_shape=(SDS((ts, n, B_CH), F32), SDS((CONV_W - 1, n, B_CH), F32)),
        grid=(B_CH // cb,),
        in_specs=[blk(L0_U), blk(L0_B), blk(L0_C), BS((CONV_W - 1, n, cb), lambda j: (0, 0, j)),
                  BS((CONV_W, cb), lambda j: (0, j))],
        out_specs=(BS((ts, n, cb), lambda j: (0, 0, j)), BS((CONV_W - 1, n, cb), lambda j: (0, 0, j))),
        compiler_params=_cp(1),
    )(z3, z3, z3, state_t, w_conv)


def _pool_s_kernel(u_ref, st_ref, wp_ref, ps_ref, y_ref, *, past):
    ts, n, _ = u_ref.shape
    hist = st_ref.shape[0]
    for g, w in enumerate(POOL_WINDOWS):
        @pl.when(pl.program_id(0) == g)
        def _(w=w):
            ext = [st_ref[e] for e in range(hist)] + [u_ref[t] for t in range(ts)]
            ds = []
            for t in range(ts):
                acc = ext[hist + t]
                for i in range(1, w):
                    acc = acc + ext[hist + t - i]
                ds.append(acc / float(min(w, past + t + 1)) - ext[hist + t])
            y = jnp.dot(jnp.concatenate(ds, axis=0).astype(BF16), wp_ref[...], preferred_element_type=F32) * ps_ref[...]
            for t in range(ts):
                y_ref[t] = y[t * n:(t + 1) * n]


def _pool_sample(z3, state_t, w_pool, pool_scale, past):
    ts, n, _ = z3.shape
    hist = state_t.shape[0]
    ng = len(POOL_WINDOWS)
    return pl.pallas_call(
        functools.partial(_pool_s_kernel, past=past),
        out_shape=SDS((ts, n, C_CH), F32),
        grid=(ng,),
        in_specs=[BS((ts, n, C_GRP), lambda g: (0, 0, L1_U // C_GRP + g)),
                  BS((hist, n, C_GRP), lambda g: (0, 0, g)),
                  BS((None, C_GRP, C_GRP), lambda g: (g, 0, 0)),
                  BS((1, C_GRP), lambda g: (0, g))],
        out_specs=BS((ts, n, C_GRP), lambda g: (0, 0, g)),
        compiler_params=_cp(1),
    )(z3, state_t, w_pool, pool_scale.reshape(1, C_CH))


def _win_s_kernel(q_ref, st_ref, kn_ref, *rest, nb, n_kv, g_heads, hd, win, ts, slopes, has_sink, tall):
    if has_sink:
        sink_ref, o_ref = rest
    else:
        (o_ref,) = rest
    rows = g_heads * ts
    nk = win + LANES
    ti = lax.broadcasted_iota(jnp.int32, (rows, nk), 0) % ts
    dist = ti + win - lax.broadcasted_iota(jnp.int32, (rows, nk), 1)
    valid = (dist >= 0) & (dist < win)
    distf = dist.astype(F32)
    zpad = jnp.zeros((LANES - ts, hd), F32)

    def rows_of(i, slab):
        cols = slice(slab * hd, (slab + 1) * hd)
        st = st_ref[i, pl.ds(slab, win, stride=2 * n_kv), :] if tall else st_ref[i, :, cols]
        return jnp.concatenate([st, kn_ref[i * ts:(i + 1) * ts, cols], zpad], axis=0).astype(BF16)

    pairs = [(i, kv) for i in range(nb) for kv in range(n_kv)]
    qks = [_nt_dot(_stack_heads(q_ref, kv, g_heads, hd, hd ** -0.5, i * ts, ts), rows_of(i, kv)) for i, kv in pairs]
    pds = []
    for (i, kv), qk in zip(pairs, qks):
        p, den = _masked_softmax(qk - _col(slopes[kv], ts) * distf, valid, sink_ref[kv] if has_sink else None)
        pds.append((p.astype(BF16), den))
    for (i, kv), (p, den) in zip(pairs, pds):
        o = jnp.dot(p, rows_of(i, n_kv + kv), preferred_element_type=F32) / den
        _unstack_heads(o_ref, o, kv, g_heads, hd, ts, i * ts)


def _win_sample(q_nm, state, kv_new_nm, n, ts, n_kv, g_heads, hd, slopes, sinks=None):
    win = state.shape[1]
    nq = n_kv * g_heads * hd
    wkv = 2 * n_kv * hd
    tall = hd == LANES
    state = state.reshape(n, win * 2 * n_kv, hd) if tall else state.reshape(n, win, wkv)
    nb = 4 if n % 4 == 0 else 1
    in_specs = [BS((nb * ts, nq), lambda i: (i, 0)), BS((nb,) + state.shape[1:], lambda i: (i, 0, 0)),
                BS((nb * ts, wkv), lambda i: (i, 0))]
    args = [q_nm, state, kv_new_nm]
    if sinks is not None:
        sink_rows = jnp.repeat(sinks.astype(F32).reshape(n_kv, g_heads), ts, axis=1).reshape(n_kv, g_heads * ts, 1)
        in_specs.append(BS((n_kv, g_heads * ts, 1), lambda i: (0, 0, 0)))
        args.append(sink_rows)
    return pl.pallas_call(
        functools.partial(_win_s_kernel, nb=nb, n_kv=n_kv, g_heads=g_heads, hd=hd, win=win, ts=ts, slopes=slopes,
                          has_sink=sinks is not None, tall=tall),
        out_shape=SDS((n * ts, nq), F32),
        grid=(n // nb,),
        in_specs=in_specs,
        out_specs=BS((nb * ts, nq), lambda i: (i, 0)),
        compiler_params=_cp(1),
    )(*args)


def _cmpsel_s_kernel(q_ref, kc_ref, vc_ref, a_ref, oc_ref, sel_ref, *, nb, ts, nc, n_past, past, slopes):
    rows = A_G * ts
    qpos = past + lax.broadcasted_iota(jnp.int32, (rows, nc), 0) % ts
    end = lax.broadcasted_iota(jnp.int32, (rows, nc), 1) * CMP_STRIDE + (2 * CMP_STRIDE - 1)
    dist = qpos - end
    valid = dist >= 0
    distf = dist.astype(F32)
    ns_pad = a_ref.shape[1]
    lane_ok = lax.broadcasted_iota(jnp.int32, (ts, ns_pad), 1) <= n_past
    scores = []
    for i in range(nb):
        for kv in range(A_KV):
            q = _stack_heads(q_ref, kv, A_G, A_HD, A_HD ** -0.5, i * ts, ts)
            kc = kc_ref[i, :, kv * A_HD:(kv + 1) * A_HD].astype(BF16)
            vc = vc_ref[i, :, kv * A_HD:(kv + 1) * A_HD].astype(BF16)
            s = _nt_dot(q, kc) - _col(slopes[kv], ts) * distf
            p, den = _masked_softmax(s, valid)
            pn = p / den
            o = jnp.dot(pn.astype(BF16), vc, preferred_element_type=F32)
            _unstack_heads(oc_ref, o, kv, A_G, A_HD, ts, i * ts)
            psum = pn[0:ts] + pn[ts:2 * ts] + pn[2 * ts:3 * ts] + pn[3 * ts:4 * ts]
            score = _select_scores(_split3_dot(psum, a_ref[...]), past)
            scores.append(jnp.where(lane_ok, score, -2.0))
    sel = _topk_mask(jnp.concatenate(scores, axis=0), SEL_TOPK)
    for i in range(nb):
        for kv in range(A_KV):
            r0 = (i * A_KV + kv) * ts
            sel_ref[i, :, kv * n_past:(kv + 1) * n_past] = sel[r0:r0 + ts, 0:n_past]


def _cmpsel_sample(q_nm, k_c, v_c, n, ts, past, slopes):
    nc = k_c.shape[1]
    n_past = past // SEL_BLOCK
    ns_pad = -(-(n_past + 1) // LANES) * LANES
    a = _sel_matrix(nc, ns_pad)
    nb = 8
    return pl.pallas_call(
        functools.partial(_cmpsel_s_kernel, nb=nb, ts=ts, nc=nc, n_past=n_past, past=past, slopes=slopes),
        out_shape=(SDS((n * ts, A_Q), F32), SDS((n, ts, A_KV * n_past), F32)),
        grid=(n // nb,),
        in_specs=[BS((nb * ts, A_Q), lambda i: (i, 0)),
                  BS((nb, nc, A_KV * A_HD), lambda i: (i, 0, 0)),
                  BS((nb, nc, A_KV * A_HD), lambda i: (i, 0, 0)),
                  BS((nc, ns_pad), lambda i: (0, 0))],
        out_specs=(BS((nb * ts, A_Q), lambda i: (i, 0)), BS((nb, ts, A_KV * n_past), lambda i: (i, 0, 0))),
        compiler_params=_cp(1),
    )(q_nm, k_c, v_c, a)


def _sel_s_kernel(pt_ref, q_ref, sel_ref, kn_ref, e_ref, *rest, n_pages, page, ts, n_past, past, slopes):
    pages = rest[:n_pages]
    o_ref, kv_ref = rest[n_pages:]
    n_slab = 2 * A_KV
    for p in range(n_pages):
        for s in range(n_slab):
            kv_ref[p * page:(p + 1) * page, s * A_HD:(s + 1) * A_HD] = (
                pages[p][pl.ds(s, page, stride=n_slab), :].astype(BF16))
    rows = A_G * ts
    ti_p = lax.broadcasted_iota(jnp.int32, (rows, past), 0) % ts
    dist_p = (past + ti_p - lax.broadcasted_iota(jnp.int32, (rows, past), 1)).astype(F32)
    ti_n = lax.broadcasted_iota(jnp.int32, (rows, LANES), 0) % ts
    dist_n = ti_n - lax.broadcasted_iota(jnp.int32, (rows, LANES), 1)
    valid_n = dist_n >= 0
    zpad = jnp.zeros((LANES - ts, A_HD), F32)
    for kv in range(A_KV):
        q = _stack_heads(q_ref, kv, A_G, A_HD, A_HD ** -0.5)
        slope = _col(slopes[kv], ts)
        kc = slice(kv * A_HD, (kv + 1) * A_HD)
        vc = slice((A_KV + kv) * A_HD, (A_KV + kv + 1) * A_HD)
        selk = sel_ref[:, kv * n_past:(kv + 1) * n_past].astype(BF16)
        mk = jnp.dot(jnp.concatenate([selk] * A_G, axis=0), e_ref[...], preferred_element_type=F32)
        valid_p = mk > 0.5
        s_p = jnp.where(valid_p, _nt_dot(q, kv_ref[:, kc]) - slope * dist_p, NEG)
        k_n = jnp.concatenate([kn_ref[:, kc], zpad], axis=0).astype(BF16)
        v_n = jnp.concatenate([kn_ref[:, vc], zpad], axis=0).astype(BF16)
        s_n = jnp.where(valid_n, _nt_dot(q, k_n) - slope * dist_n.astype(F32), NEG)
        m = jnp.maximum(jnp.max(s_p, axis=1, keepdims=True), jnp.max(s_n, axis=1, keepdims=True))
        p_p = jnp.where(valid_p, jnp.exp(s_p - m), 0.0)
        p_n = jnp.where(valid_n, jnp.exp(s_n - m), 0.0)
        den = jnp.sum(p_p, axis=1, keepdims=True) + jnp.sum(p_n, axis=1, keepdims=True)
        o = jnp.dot(p_p.astype(BF16), kv_ref[:, vc], preferred_element_type=F32)
        o = o + jnp.dot(p_n.astype(BF16), v_n, preferred_element_type=F32)
        _unstack_heads(o_ref, o / jnp.maximum(den, TINY), kv, A_G, A_HD, ts)


def _sel_sample(q_nm, sel, kvs_new_nm, cache_sel, page_table, n, ts, past, slopes):
    n_pages = page_table.shape[1]
    page = cache_sel.shape[1]
    wkv = 2 * A_KV * A_HD
    n_past = past // SEL_BLOCK
    n_slab = 2 * A_KV
    pages = cache_sel.reshape(cache_sel.shape[0], page * n_slab, A_HD)
    e = np.zeros((n_past, past), np.float32)
    e[np.arange(past) // SEL_BLOCK, np.arange(past)] = 1.0
    in_specs = [BS((ts, A_Q), lambda i, pt: (i, 0)),
                BS((None, ts, A_KV * n_past), lambda i, pt: (i, 0, 0)),
                BS((ts, wkv), lambda i, pt: (i, 0)),
                BS((n_past, past), lambda i, pt: (0, 0))]
    in_specs += [BS((None, page * n_slab, A_HD), (lambda i, pt, p=p: (pt[i, p], 0, 0))) for p in range(n_pages)]
    return pl.pallas_call(
        functools.partial(_sel_s_kernel, n_pages=n_pages, page=page, ts=ts, n_past=n_past, past=past, slopes=slopes),
        out_shape=SDS((n * ts, A_Q), F32),
        grid_spec=pltpu.PrefetchScalarGridSpec(
            num_scalar_prefetch=1, grid=(n,), in_specs=in_specs,
            out_specs=BS((ts, A_Q), lambda i, pt: (i, 0)),
            scratch_shapes=[pltpu.VMEM((past, wkv), BF16)]),
        compiler_params=_cp(1),
    )(page_table, q_nm, sel, kvs_new_nm, jnp.asarray(e, BF16), *([pages] * n_pages))


def _to_nm(a_tm, n, ts):
    w = a_tm.shape[1]
    return a_tm.reshape(ts, n, w).transpose(1, 0, 2).reshape(n * ts, w)


def _to_tm(a_nm, n, ts):
    w = a_nm.shape[1]
    return a_nm.reshape(n, ts, w).transpose(1, 0, 2).reshape(ts * n, w)


def kernel(x_prompt, x_sample, cache_cmp_kv, cache_sel_kv, state_win_kv, state_conv, state_pool, state_swa_kv, page_table, c_prompt, c_sample, norm_g, w_ada, b_ada, w_ffn_in, w_ffn_out, final_g, w_in0, w_out0, w_cmp1, pe_cmp, w_cmp2, w_conv, w_in1, w_out1, attn_sinks, w_pool, pool_scale):
    b, t, d = x_prompt.shape
    n, ts, _ = x_sample.shape
    page = cache_cmp_kv.shape[1]
    past = page_table.shape[1] * page
    assert t % 512 == 0 or t in (128, 256), t
    assert past % SEL_BLOCK == 0 and ts <= CMP_STRIDE and past >= POOL_STATE and page % (2 * CMP_STRIDE) == 0
    assert page_table.shape[1] % 2 == 0 and ts % 8 == 0 and n % 8 == 0

    slopes_a = _slopes(A_HEADS, A_KV)
    slopes_d = _slopes(D_HEADS, D_KV)

    w0 = jnp.concatenate([w_in0[:, :A_Q], w_in0[:, A_Q + 3 * A_KVW + A_GATES:], w_in0[:, A_Q:A_Q + 3 * A_KVW],
                          w_in0[:, A_Q + 3 * A_KVW:A_Q + 3 * A_KVW + A_GATES],
                          jnp.zeros((d, LANES - A_GATES), F32)], axis=1).astype(BF16)
    w1 = jnp.concatenate([w_in1[:, :D_Q], w_in1[:, D_Q + 2 * D_KVW:], w_in1[:, D_Q:D_Q + 2 * D_KVW]], axis=1).astype(BF16)
    wo0a, wo0b = w_out0[:A_Q].astype(BF16), w_out0[A_Q:].astype(BF16)
    wo1a, wo1b = w_out1[:D_Q].astype(BF16), w_out1[D_Q:].astype(BF16)
    wf_in, wf_out = w_ffn_in.astype(BF16), w_ffn_out.astype(BF16)
    wc1, pe, wc2 = _cmp_weights(w_cmp1, pe_cmp, w_cmp2)
    wp = w_pool.astype(BF16)

    m_all = -(-(n + b) // 8) * 8
    c_all = jnp.concatenate([c_sample, c_prompt, jnp.zeros((m_all - n - b, d), F32)], axis=0)
    mods = _ada(c_all, w_ada, b_ada)

    tm_p = min(512, t)
    rows_s = n * ts
    tm_s = rows_s if rows_s <= 512 else 512
    assert tm_s % n == 0 and rows_s % tm_s == 0

    xp = x_prompt.reshape(b * t, d)
    xs = x_sample.transpose(1, 0, 2).reshape(rows_s, d)

    gp = _prompt_group(mods[0, n:n + b], b, t, d, tm_p)
    gs = _sample_group(mods[0, :n], n, ts, d, tm_s)
    xp = _ffn(xp, gp, 0, norm_g[0, 0], wf_in, wf_out, 0, 0)
    xs = _ffn(xs, gs, 0, norm_g[0, 0], wf_in, wf_out, 0, 0)
    zp = _proj(xp, gp, 1, norm_g[0, 1], w0)
    zs = _proj(xs, gs, 1, norm_g[0, 1], w0)

    cmp_p = zp[:, L0_KVC:L0_KVC + A_KVW]
    sel_p = zp[:, L0_KVS:L0_KVS + A_KVW]
    win_p = zp[:, L0_KVW:L0_KVW + A_KVW]
    kc_p, vc_p = _compress_prompt(cmp_p, b, t, wc1, pe, wc2)
    oc_p, selmask_p = _cmpsel_prompt(zp, kc_p, vc_p, b, t, slopes_a)
    os_p = _sel_prompt(zp, selmask_p, b, t, slopes_a)
    ow_p = _band_attn(zp, b, t, A_KV, A_G, A_HD, A_WIN, L0_KVW, slopes_a)
    yb_p, vtail_p = _conv_prompt(zp, b, t, w_conv)
    xp = _out0(xp, gp, oc_p, os_p, ow_p, zp, yb_p, wo0a, wo0b)

    q_nm = _to_nm(zs[:, :A_Q], n, ts)
    cmp_s = _to_nm(zs[:, L0_KVC:L0_KVC + A_KVW], n, ts)
    sel_s = _to_nm(zs[:, L0_KVS:L0_KVS + A_KVW], n, ts)
    win_s = _to_nm(zs[:, L0_KVW:L0_KVW + A_KVW], n, ts)
    kc_s, vc_s = _compress_sample(cache_cmp_kv, page_table, cmp_s.reshape(n, ts, A_KVW), wc1, pe, wc2)
    oc_s, selmask_s = _cmpsel_sample(q_nm, kc_s, vc_s, n, ts, past, slopes_a)
    os_s = _sel_sample(q_nm, selmask_s, sel_s, cache_sel_kv, page_table, n, ts, past, slopes_a)
    ow_s = _win_sample(q_nm, state_win_kv, win_s, n, ts, A_KV, A_G, A_HD, slopes_a)
    zs3 = zs.reshape(ts, n, L0_N)
    yb_s, conv_state_t = _conv_sample(zs3, state_conv.transpose(1, 0, 2), w_conv)
    xs = _out0(xs, gs, _to_tm(oc_s, n, ts), _to_tm(os_s, n, ts), _to_tm(ow_s, n, ts), zs,
               yb_s.reshape(rows_s, B_CH), wo0a, wo0b)

    xp = _ffn(xp, gp, 2, norm_g[0, 2], wf_in, wf_out, 0, 1)
    xs = _ffn(xs, gs, 2, norm_g[0, 2], wf_in, wf_out, 0, 1)

    gp = _prompt_group(mods[1, n:n + b], b, t, d, tm_p)
    gs = _sample_group(mods[1, :n], n, ts, d, tm_s)
    xp = _ffn(xp, gp, 0, norm_g[1, 0], wf_in, wf_out, 1, 0)
    xs = _ffn(xs, gs, 0, norm_g[1, 0], wf_in, wf_out, 1, 0)
    z1p = _proj(xp, gp, 1, norm_g[1, 1], w1)
    z1s = _proj(xs, gs, 1, norm_g[1, 1], w1)

    od_p = _band_attn(z1p, b, t, D_KV, D_G, D_HD, D_WIN, L1_KV, slopes_d, attn_sinks)
    yc_p, utail_p = _pool_prompt(z1p, b, t, wp, pool_scale)
    xp = _out1(xp, gp, od_p, yc_p, wo1a, wo1b)

    q1_nm = _to_nm(z1s[:, :D_Q], n, ts)
    kv1_nm = _to_nm(z1s[:, L1_KV:L1_KV + 2 * D_KVW], n, ts)
    u1_nm = _to_nm(z1s[:, L1_U:L1_U + C_CH], n, ts)
    od_s = _win_sample(q1_nm, state_swa_kv, kv1_nm, n, ts, D_KV, D_G, D_HD, slopes_d, attn_sinks)
    yc_s = _pool_sample(z1s.reshape(ts, n, L1_N), state_pool.transpose(1, 0, 2), wp, pool_scale, past)
    xs = _out1(xs, gs, _to_tm(od_s, n, ts), yc_s.reshape(rows_s, C_CH), wo1a, wo1b)

    xp = _ffn(xp, gp, 2, norm_g[1, 2], wf_in, wf_out, 1, 1, final_g)
    xs = _ffn(xs, gs, 2, norm_g[1, 2], wf_in, wf_out, 1, 1, final_g)

    y_prompt = xp.reshape(b, t, d)
    y_sample = xs.reshape(ts, n, d).transpose(1, 0, 2)

    kv5 = (2, A_KV, A_HD)
    cmp_kv_prompt = cmp_p.reshape(b, t, *kv5)
    sel_kv_prompt = sel_p.reshape(b, t, *kv5)
    win_kv_prompt = win_p.reshape(b, t, *kv5)[:, t - A_WIN:]
    cmp_kv_sample = cmp_s.reshape(n, ts, *kv5)
    sel_kv_sample = sel_s.reshape(n, ts, *kv5)
    win_kv_sample = jnp.concatenate([state_win_kv, win_s.reshape(n, ts, *kv5)], axis=1)[:, ts:]
    conv_prompt = vtail_p[:, 8 - (CONV_W - 1):]
    conv_sample = conv_state_t.transpose(1, 0, 2)
    pool_prompt = utail_p[:, 16 - POOL_STATE:]
    pool_sample = jnp.concatenate([state_pool, u1_nm.reshape(n, ts, C_CH)], axis=1)[:, ts:]
    swa5 = (2, D_KV, D_HD)
    swa_kv_prompt = z1p[:, L1_KV:L1_KV + 2 * D_KVW].reshape(b, t, *swa5)[:, t - D_WIN:]
    swa_kv_sample = jnp.concatenate([state_swa_kv, kv1_nm.reshape(n, ts, *swa5)], axis=1)[:, ts:]
    return (y_prompt, y_sample, cmp_kv_prompt, cmp_kv_sample, sel_kv_prompt, sel_kv_sample, win_kv_prompt,
            win_kv_sample, conv_prompt, conv_sample, pool_prompt, pool_sample, swa_kv_prompt, swa_kv_sample)
```
